```python
import math
import jax
import jax.numpy as jnp
from jax import lax
import numpy as np

D_MODEL = 2048
BATCH = 1
SEQ = 16384
DEPTH = 2

MEM_LEN = 256
EPS = 1e-6
RET_HEADS = 8
RET_DK = 128
RET_DV = 128
RET_CHUNK = 128
ROPE_BASE = 10000.0
CONV_WIDTH = 1024
CONV_K = 3
DIFF_HEADS = 4
DIFF_D = 128
Q_BLOCK = 128
T5_BUCKETS = 32
T5_MAX_DIST = 128
X_HEADS = 4
X_HEAD_DIM = 128
D_FF = 7168
N_EXPERTS = 8
TOP_K = 2
N_BRANCH = 3
BRANCH_W = 1024
N_DENSE = (DEPTH + 1) // 2
N_MOE = DEPTH // 2
IN_WIDTHS = (RET_HEADS * RET_DK, RET_HEADS * RET_DK, RET_HEADS * RET_DV, RET_HEADS * RET_DV,
             CONV_WIDTH, CONV_WIDTH, CONV_WIDTH,
             DIFF_HEADS * 2 * DIFF_D, DIFF_HEADS * 2 * DIFF_D, DIFF_HEADS * 2 * DIFF_D,
             N_BRANCH * D_MODEL)
IN_TOTAL = sum(IN_WIDTHS)

kernel_name = 'hybrid_gated_retention_conv_diffattn_moe'


def rmsnorm(x, g=None):
    xf = x.astype(jnp.float32)
    y = xf * lax.rsqrt(jnp.mean(xf * xf, axis=-1, keepdims=True) + EPS)
    if g is not None:
        y = y * g.astype(jnp.float32)
    return y.astype(x.dtype)


def rope(x, pos):
    half = x.shape[-1] // 2
    inv = ROPE_BASE ** (-jnp.arange(half, dtype=jnp.float32) / half)
    ang = pos[:, None].astype(jnp.float32) * inv[None, :]
    cos = jnp.cos(ang)[None, :, None, :]
    sin = jnp.sin(ang)[None, :, None, :]
    x1, x2 = x[..., :half], x[..., half:]
    return jnp.concatenate([x1 * cos - x2 * sin, x1 * sin + x2 * cos], axis=-1).astype(x.dtype)


def retention(q, k, v):
    B, S, H, dk = q.shape
    dv = v.shape[-1]
    C = RET_CHUNK
    N = S // C
    dt = q.dtype
    log_g = jnp.log1p(-(2.0 ** (-5.0 - jnp.arange(H, dtype=jnp.float32))))
    q = q * (dk ** -0.5)
    qc = q.reshape(B, N, C, H, dk)
    kc = k.reshape(B, N, C, H, dk)
    vc = v.reshape(B, N, C, H, dv)
    idx = jnp.arange(C, dtype=jnp.float32)
    dist = idx[:, None] - idx[None, :]
    decay_intra = jnp.where(dist[None] >= 0,
                            jnp.exp(jnp.maximum(dist, 0.0)[None] * log_g[:, None, None]), 0.0).astype(dt)
    scores = jnp.einsum('bnqhd,bnkhd->bnhqk', qc, kc) * decay_intra
    intra = jnp.einsum('bnhqk,bnkhe->bnqhe', scores, vc)
    k_decay = jnp.exp((C - 1 - idx)[:, None] * log_g[None, :]).astype(dt)
    kv = jnp.einsum('bnkhd,bnkhe->bnhde', kc * k_decay[None, None, :, :, None], vc)
    chunk_decay = jnp.exp(C * log_g).astype(kv.dtype)[None, :, None, None]

    def step(state, kv_n):
        return state * chunk_decay + kv_n, state

    _, prev = lax.scan(step, jnp.zeros((B, H, dk, dv), kv.dtype), jnp.moveaxis(kv, 1, 0))
    prev = jnp.moveaxis(prev, 0, 1)
    q_decay = jnp.exp((idx + 1.0)[:, None] * log_g[None, :]).astype(dt)
    cross = jnp.einsum('bnqhd,bnhde->bnqhe', qc * q_decay[None, None, :, :, None], prev)
    return (intra + cross).reshape(B, S, H, dv)


def short_conv(u, b, c, w):
    v = c * u
    y = lax.conv_general_dilated(v, w[:, None, :], window_strides=(1,), padding=[(CONV_K - 1, 0)],
                                 dimension_numbers=('NWC', 'WIO', 'NWC'),
                                 feature_group_count=v.shape[-1])
    return b * y


def t5_bucket(dist):
    max_exact = T5_BUCKETS // 2
    d = jnp.maximum(dist, 0)
    df = jnp.maximum(d, 1).astype(jnp.float32)
    large = max_exact + (jnp.log(df / max_exact) / math.log(T5_MAX_DIST / max_exact)
                         * (T5_BUCKETS - max_exact)).astype(jnp.int32)
    large = jnp.minimum(large, T5_BUCKETS - 1)
    return jnp.where(d < max_exact, d, large)


def diff_attention(q, k, v, lam, lam_init, subln_g, t5_bias):
    B, S, H = q.shape[0], q.shape[1], q.shape[2]
    NQ = S // Q_BLOCK
    scale = DIFF_D ** -0.5
    qb = jnp.moveaxis(q.reshape(B, NQ, Q_BLOCK, H, 2, DIFF_D), 1, 0)
    kpos = jnp.arange(S, dtype=jnp.int32)
    bias_tab = t5_bias.T.astype(jnp.float32)

    def block(args):
        q_blk, i = args
        qpos = i * Q_BLOCK + jnp.arange(Q_BLOCK, dtype=jnp.int32)
        dist = qpos[:, None] - kpos[None, :]
        bias = bias_tab[:, t5_bucket(dist)]
        logits = jnp.einsum('bqhmd,bkhmd->bmhqk', q_blk, k).astype(jnp.float32) * scale + bias[None, None]
        logits = jnp.where(dist >= 0, logits, -1e30)
        p = jax.nn.softmax(logits, axis=-1)
        a = p[:, 0] - lam * p[:, 1]
        return jnp.einsum('bhqk,bkhe->bqhe', a.astype(v.dtype), v)

    o = lax.map(block, (qb, jnp.arange(NQ, dtype=jnp.int32)))
    o = jnp.moveaxis(o, 0, 1).reshape(B, S, H, 2 * DIFF_D)
    o = rmsnorm(o, subln_g) * (1.0 - lam_init)
    return o.reshape(B, S, H * 2 * DIFF_D)


def cross_attention(h, mem_n, wq, wkv, wo):
    B, S, _ = h.shape
    M = mem_n.shape[1]
    q = (h @ wq).reshape(B, S, X_HEADS, X_HEAD_DIM)
    kv = (mem_n @ wkv).reshape(B, M, 2, X_HEADS, X_HEAD_DIM)
    k, v = kv[:, :, 0], kv[:, :, 1]
    logits = jnp.einsum('bshd,bmhd->bhsm', q, k).astype(jnp.float32) * (X_HEAD_DIM ** -0.5)
    p = jax.nn.softmax(logits, axis=-1)
    o = jnp.einsum('bhsm,bmhd->bshd', p.astype(v.dtype), v).reshape(B, S, X_HEADS * X_HEAD_DIM)
    return o @ wo


def swiglu(h, wg, wu, wd):
    return (jax.nn.silu(h @ wg) * (h @ wu)) @ wd


def moe_swiglu(h, w_router, wg, wu, wd):
    B, S, D = h.shape
    hf = h.reshape(B * S, D)
    logits = (hf @ w_router).astype(jnp.float32)
    top_v, top_i = lax.top_k(logits, TOP_K)
    gate = jax.nn.softmax(top_v, axis=-1)
    combine = jnp.sum(jax.nn.one_hot(top_i, N_EXPERTS, dtype=jnp.float32) * gate[..., None], axis=1)
    out = jnp.zeros_like(hf)
    for e in range(N_EXPERTS):
        ye = swiglu(hf, wg[e], wu[e], wd[e])
        out = out + combine[:, e:e + 1].astype(hf.dtype) * ye
    return out.reshape(B, S, D)


def setup_inputs(seed: int = 0) -> dict:
    key = jax.random.key(seed)
    ks = jax.random.split(key, 32)
    f32 = jnp.float32

    def nrm(k, shape, fan_in):
        return jax.random.normal(k, shape, f32) * (fan_in ** -0.5)

    def gain(k, shape):
        return 1.0 + 0.02 * jax.random.normal(k, shape, f32)

    return {
        'x': jax.random.normal(ks[0], (BATCH, SEQ, D_MODEL), f32),
        'mem': jax.random.normal(ks[1], (BATCH, MEM_LEN, D_MODEL), f32),
        't5_bias': 0.5 * jax.random.normal(ks[2], (T5_BUCKETS, DIFF_HEADS), f32),
        'w_in': nrm(ks[3], (DEPTH, D_MODEL, IN_TOTAL), D_MODEL),
        'w_conv': nrm(ks[4], (DEPTH, CONV_K, CONV_WIDTH), CONV_K),
        'diff_lambda': 0.1 * jax.random.normal(ks[5], (DEPTH, 4, DIFF_D), f32),
        'diff_subln': gain(ks[6], (DEPTH, 2 * DIFF_D)),
        'w_branch': nrm(ks[7], (DEPTH, N_BRANCH, BRANCH_W, D_MODEL), BRANCH_W),
        'w_mix_out': nrm(ks[8], (DEPTH, D_MODEL, D_MODEL), D_MODEL),
        'w_xq': nrm(ks[9], (DEPTH, D_MODEL, X_HEADS * X_HEAD_DIM), D_MODEL),
        'w_xkv': nrm(ks[10], (DEPTH, D_MODEL, 2 * X_HEADS * X_HEAD_DIM), D_MODEL),
        'w_xo': nrm(ks[11], (DEPTH, X_HEADS * X_HEAD_DIM, D_MODEL), X_HEADS * X_HEAD_DIM),
        'w_ffn_gate': nrm(ks[12], (N_DENSE, D_MODEL, D_FF), D_MODEL),
        'w_ffn_up': nrm(ks[13], (N_DENSE, D_MODEL, D_FF), D_MODEL),
        'w_ffn_down': nrm(ks[14], (N_DENSE, D_FF, D_MODEL), D_FF),
        'w_router': nrm(ks[15], (N_MOE, D_MODEL, N_EXPERTS), D_MODEL),
        'w_exp_gate': nrm(ks[16], (N_MOE, N_EXPERTS, D_MODEL, D_FF), D_MODEL),
        'w_exp_up': nrm(ks[17], (N_MOE, N_EXPERTS, D_MODEL, D_FF), D_MODEL),
        'w_exp_down': nrm(ks[18], (N_MOE, N_EXPERTS, D_FF, D_MODEL), D_FF),
        'g_pre_mix': gain(ks[19], (DEPTH, D_MODEL)),
        'g_post_mix': gain(ks[20], (DEPTH, D_MODEL)),
        'g_pre_xattn': gain(ks[21], (DEPTH, D_MODEL)),
        'g_mem': gain(ks[22], (DEPTH, D_MODEL)),
        'g_post_xattn': gain(ks[23], (DEPTH, D_MODEL)),
        'g_pre_ffn': gain(ks[24], (DEPTH, D_MODEL)),
        'g_post_ffn': gain(ks[25], (DEPTH, D_MODEL)),
    }


def reference(x, mem, t5_bias, w_in, w_conv, diff_lambda, diff_subln, w_branch, w_mix_out,
              w_xq, w_xkv, w_xo, w_ffn_gate, w_ffn_up, w_ffn_down, w_router, w_exp_gate,
              w_exp_up, w_exp_down, g_pre_mix, g_post_mix, g_pre_xattn, g_mem, g_post_xattn,
              g_pre_ffn, g_post_ffn):
    B, S, D = x.shape
    pos = jnp.arange(S, dtype=jnp.int32)
    split_pts = [int(p) for p in np.cumsum(IN_WIDTHS)[:-1]]
    for l in range(DEPTH):
        h = rmsnorm(x, g_pre_mix[l])
        (rq, rk, rv, rg, cb, cc, cu, dq, dk, dv, gl) = jnp.split(h @ w_in[l], split_pts, axis=-1)
        rq = rope(rq.reshape(B, S, RET_HEADS, RET_DK), pos)
        rk = rope(rk.reshape(B, S, RET_HEADS, RET_DK), pos)
        y_a = retention(rq, rk, rv.reshape(B, S, RET_HEADS, RET_DV))
        y_a = jax.nn.silu(rg) * rmsnorm(y_a).reshape(B, S, RET_HEADS * RET_DV)
        y_b = short_conv(cu, cb, cc, w_conv[l])
        lq1, lk1, lq2, lk2 = diff_lambda[l]
        lam_init = 0.8 - 0.6 * math.exp(-0.3 * l)
        lam = (jnp.exp(jnp.sum(lq1 * lk1).astype(jnp.float32))
               - jnp.exp(jnp.sum(lq2 * lk2).astype(jnp.float32)) + lam_init)
        y_c = diff_attention(dq.reshape(B, S, DIFF_HEADS, 2, DIFF_D), dk.reshape(B, S, DIFF_HEADS, 2, DIFF_D),
                             dv.reshape(B, S, DIFF_HEADS, 2 * DIFF_D), lam, lam_init, diff_subln[l], t5_bias)
        ys = jnp.stack([y_a, y_b, y_c], axis=2)
        z = jnp.einsum('bsnc,ncd->bsnd', ys, w_branch[l])
        gates = jax.nn.sigmoid(gl.reshape(B, S, N_BRANCH, D))
        merged = jnp.sum(gates * z, axis=2)
        x = x + rmsnorm(merged @ w_mix_out[l], g_post_mix[l])
        h = rmsnorm(x, g_pre_xattn[l])
        mem_n = rmsnorm(mem, g_mem[l])
        x = x + rmsnorm(cross_attention(h, mem_n, w_xq[l], w_xkv[l], w_xo[l]), g_post_xattn[l])
        h = rmsnorm(x, g_pre_ffn[l])
        if l % 2 == 0:
            f = swiglu(h, w_ffn_gate[l // 2], w_ffn_up[l // 2], w_ffn_down[l // 2])
        else:
            f = moe_swiglu(h, w_router[l // 2], w_exp_gate[l // 2], w_exp_up[l // 2], w_exp_down[l // 2])
        x = x + rmsnorm(f, g_post_ffn[l])
    return x
```

```python
import functools
import math

import numpy as np
import jax
import jax.numpy as jnp
from jax import lax
from jax.experimental import pallas as pl
from jax.experimental.pallas import tpu as pltpu

F32 = jnp.float32
BF16 = jnp.bfloat16

EPS = 1e-6
RET_HEADS = 8
RET_D = 128
RET_CHUNK = 128
ROPE_BASE = 10000.0
CONV_WIDTH = 1024
CONV_K = 3
DIFF_HEADS = 4
DIFF_D = 128
T5_BUCKETS = 32
T5_MAX_DIST = 128
X_HEADS = 4
X_HEAD_DIM = 128
N_EXPERTS = 8
N_BRANCH = 3
BRANCH_W = 1024

V7X_VMEM_BYTES = 64 * 1024 * 1024
VMEM_LIMIT = V7X_VMEM_BYTES - 8 * 1024 * 1024
LANES = 128
NEG_BIG = -1e30


def _params(*semantics):
    return pltpu.CompilerParams(dimension_semantics=semantics, vmem_limit_bytes=VMEM_LIMIT)


def _rmsnorm(x, g=None):
    y = x * lax.rsqrt(jnp.mean(x * x, axis=-1, keepdims=True) + EPS)
    return y if g is None else y * g


def _dot(a, b):
    return jnp.dot(a, b, preferred_element_type=F32)


def _dot_nt(a, b):
    return lax.dot_general(a, b, (((1,), (1,)), ((), ())), preferred_element_type=F32)


def _in_proj_kernel(x_ref, g_ref, w_ref, o_ref, hn_ref):
    @pl.when(pl.program_id(1) == 0)
    def _():
        hn_ref[...] = _rmsnorm(x_ref[...], g_ref[...]).astype(BF16)

    o_ref[...] = _dot(hn_ref[...], w_ref[...]).astype(BF16)


def _in_projection(x, g, w, *, tm=1024, tn=1024):
    S, D = x.shape
    N = w.shape[1]
    return pl.pallas_call(
        _in_proj_kernel,
        grid=(S // tm, N // tn),
        in_specs=[pl.BlockSpec((tm, D), lambda i, j: (i, 0)),
                  pl.BlockSpec((1, D), lambda i, j: (0, 0)),
                  pl.BlockSpec((D, tn), lambda i, j: (0, j))],
        out_specs=pl.BlockSpec((tm, tn), lambda i, j: (i, j)),
        out_shape=jax.ShapeDtypeStruct((S, N), BF16),
        scratch_shapes=[pltpu.VMEM((tm, D), BF16)],
        compiler_params=_params("parallel", "arbitrary"),
        name="in_projection",
    )(x, g, w)


def _retention_consts():
    C, H = RET_CHUNK, RET_HEADS
    log_g = np.log1p(-(2.0 ** (-5.0 - np.arange(H, dtype=np.float64))))
    idx = np.arange(C, dtype=np.float64)
    dist = idx[:, None] - idx[None, :]
    intra = np.where(dist[None] >= 0, np.exp(np.maximum(dist, 0.0)[None] * log_g[:, None, None]), 0.0)
    kdec = np.exp((C - 1 - idx)[None, :, None] * log_g[:, None, None]) * np.ones((1, 1, RET_D))
    qdec = np.exp((idx + 1.0)[None, :, None] * log_g[:, None, None]) * np.ones((1, 1, RET_D))
    chunk_decay = tuple(float(v) for v in np.exp(C * log_g))
    return (jnp.asarray(intra, F32), jnp.asarray(qdec, F32), jnp.asarray(kdec, F32), chunk_decay)


def _rope(x, cos2, sin2):
    return x * cos2 + pltpu.roll(x, RET_D // 2, 1) * sin2


def _retention_kernel(q_ref, k_ref, v_ref, g_ref, cos_ref, sin_ref, intra_ref, qdec_ref, kdec_ref,
                      o_ref, state_ref, *, chunks, chunk_decay):
    @pl.when(pl.program_id(0) == 0)
    def _():
        state_ref[...] = jnp.zeros_like(state_ref)

    C = RET_CHUNK
    for c in range(chunks):
        rows = slice(c * C, (c + 1) * C)
        cos2 = cos_ref[rows, :]
        sin2 = sin_ref[rows, :]
        for h in range(RET_HEADS):
            cols = slice(h * RET_D, (h + 1) * RET_D)
            q = _rope(q_ref[rows, cols].astype(F32), cos2, sin2)
            k = _rope(k_ref[rows, cols].astype(F32), cos2, sin2)
            v = v_ref[rows, cols]
            state = state_ref[h]
            scores = _dot_nt(q.astype(BF16), k.astype(BF16)) * intra_ref[h]
            o = _dot(scores.astype(BF16), v) + _dot((q * qdec_ref[h]).astype(BF16), state.astype(BF16))
            kv = _dot((k * kdec_ref[h]).T.astype(BF16), v)
            state_ref[h] = state * chunk_decay[h] + kv
            gate = g_ref[rows, cols].astype(F32)
            o_ref[rows, cols] = (gate * jax.nn.sigmoid(gate) * _rmsnorm(o)).astype(BF16)


def _retention(proj, cos2, sin2, *, chunks=2):
    S = proj.shape[0]
    W = RET_HEADS * RET_D
    R = chunks * RET_CHUNK
    intra, qdec, kdec, chunk_decay = _retention_consts()
    col = lambda b: pl.BlockSpec((R, W), lambda i, b=b: (i, b))
    const = pl.BlockSpec((RET_HEADS, RET_CHUNK, RET_D), lambda i: (0, 0, 0))
    rope_spec = pl.BlockSpec((R, RET_D), lambda i: (i, 0))
    return pl.pallas_call(
        functools.partial(_retention_kernel, chunks=chunks, chunk_decay=chunk_decay),
        grid=(S // R,),
        in_specs=[col(0), col(1), col(2), col(3), rope_spec, rope_spec, const, const, const],
        out_specs=pl.BlockSpec((R, W), lambda i: (i, 0)),
        out_shape=jax.ShapeDtypeStruct((S, W), BF16),
        scratch_shapes=[pltpu.VMEM((RET_HEADS, RET_D, RET_D), F32)],
        compiler_params=_params("arbitrary"),
        name="retention",
    )(proj, proj, proj, proj, cos2, sin2, intra, qdec, kdec)


def _t5_bucket(dist):
    max_exact = T5_BUCKETS // 2
    d = jnp.maximum(dist, 0)
    df = jnp.maximum(d, 1).astype(F32)
    large = max_exact + (jnp.log(df / max_exact) / math.log(T5_MAX_DIST / max_exact)
                         * (T5_BUCKETS - max_exact)).astype(jnp.int32)
    large = jnp.minimum(large, T5_BUCKETS - 1)
    return jnp.where(d < max_exact, d, large)


def _near_bias_tiles(t5_bias, t):
    assert t >= T5_MAX_DIST
    ql = jnp.arange(t, dtype=jnp.int32)[:, None]
    kl = jnp.arange(t, dtype=jnp.int32)[None, :]
    tab = t5_bias.T.astype(F32)
    tiles = []
    for off in (0, 1):
        dist = ql + off * t - kl
        b = tab[:, _t5_bucket(dist)]
        tiles.append(jnp.where(dist[None] >= 0, b, NEG_BIG))
    return jnp.stack(tiles, axis=1)


def _diff_attn_kernel(t5_ref, lam_ref, q_ref, k_ref, v_ref, bias_ref, subg_ref, o_ref,
                      m_ref, l_ref, acc_ref, *, t, lam_init):
    h = pl.program_id(0)
    i = pl.program_id(1)
    far_bias = t5_ref[h, T5_BUCKETS - 1]
    m_ref[...] = jnp.full_like(m_ref, NEG_BIG)
    l_ref[...] = jnp.zeros_like(l_ref)
    acc_ref[...] = jnp.zeros_like(acc_ref)
    q = q_ref[...]

    def block(k, v, bias):
        for mp in range(2):
            cols = slice(mp * DIFF_D, (mp + 1) * DIFF_D)
            s = _dot_nt(q[:, cols], k[:, cols]) + bias
            m_prev = m_ref[mp]
            m_new = jnp.maximum(m_prev, jnp.max(s, axis=-1, keepdims=True))
            alpha = jnp.exp(m_prev - m_new)
            p = jnp.exp(s - m_new)
            l_ref[mp] = alpha * l_ref[mp] + jnp.sum(p, axis=-1, keepdims=True)
            acc_ref[mp] = alpha * acc_ref[mp] + _dot(p.astype(BF16), v)
            m_ref[mp] = m_new

    def far_body(j, carry):
        off = pl.multiple_of(j * t, t)
        block(k_ref[pl.ds(off, t), :], v_ref[pl.ds(off, t), :], far_bias)
        return carry

    lax.fori_loop(0, i - 1, far_body, 0)

    @pl.when(i >= 1)
    def _():
        off = pl.multiple_of((i - 1) * t, t)
        block(k_ref[pl.ds(off, t), :], v_ref[pl.ds(off, t), :], bias_ref[1])

    off = pl.multiple_of(i * t, t)
    block(k_ref[pl.ds(off, t), :], v_ref[pl.ds(off, t), :], bias_ref[0])

    lam_p = lam_ref[...]
    lam = (jnp.exp(jnp.sum(lam_p[0:1] * lam_p[1:2], axis=-1, keepdims=True))
           - jnp.exp(jnp.sum(lam_p[2:3] * lam_p[3:4], axis=-1, keepdims=True)) + lam_init)
    o = acc_ref[0] / l_ref[0] - lam * (acc_ref[1] / l_ref[1])
    o_ref[...] = (_rmsnorm(o, subg_ref[...]) * (1.0 - lam_init)).astype(BF16)


def _diff_attention(proj, t5_bias, lam_params, subln_g, lam_init, *, col0, t=512):
    S = proj.shape[0]
    HW = 2 * DIFF_D
    qb = col0 // HW
    kb = qb + DIFF_HEADS
    vb = kb + DIFF_HEADS
    bias = _near_bias_tiles(t5_bias, t)
    return pl.pallas_call(
        functools.partial(_diff_attn_kernel, t=t, lam_init=lam_init),
        grid=(DIFF_HEADS, S // t),
        in_specs=[pl.BlockSpec(memory_space=pltpu.SMEM),
                  pl.BlockSpec((4, DIFF_D), lambda h, i: (0, 0)),
                  pl.BlockSpec((t, HW), lambda h, i: (i, qb + h)),
                  pl.BlockSpec((S, HW), lambda h, i: (0, kb + h)),
                  pl.BlockSpec((S, HW), lambda h, i: (0, vb + h)),
                  pl.BlockSpec((None, 2, t, t), lambda h, i: (h, 0, 0, 0)),
                  pl.BlockSpec((1, HW), lambda h, i: (0, 0))],
        out_specs=pl.BlockSpec((t, HW), lambda h, i: (i, h)),
        out_shape=jax.ShapeDtypeStruct((S, DIFF_HEADS * HW), BF16),
        scratch_shapes=[pltpu.VMEM((2, t, 1), F32), pltpu.VMEM((2, t, 1), F32), pltpu.VMEM((2, t, HW), F32)],
        compiler_params=_params("arbitrary", "arbitrary"),
        name="diff_attention",
    )(t5_bias.T.astype(F32), lam_params, proj, proj, proj, bias, subln_g)


def _merge_kernel(ya_ref, cb_ref, cc_ref, cu_ref, hc_ref, hu_ref, wconv_ref, yc_ref,
                  g0_ref, g1_ref, g2_ref, wb_ref, o_ref, *, halo):
    i = pl.program_id(0)
    v = cc_ref[...].astype(F32) * cu_ref[...].astype(F32)
    hv = jnp.where(i > 0, hc_ref[...].astype(F32) * hu_ref[...].astype(F32), 0.0)
    prev1 = hv[halo - 1:halo, :]
    prev2 = hv[halo - 2:halo - 1, :]
    row = lax.broadcasted_iota(jnp.int32, v.shape, 0)
    v1 = jnp.where(row == 0, prev1, pltpu.roll(v, 1, 0))
    v2 = jnp.where(row == 0, prev2, jnp.where(row == 1, prev1, pltpu.roll(v, 2, 0)))
    w = wconv_ref[...]
    yb = cb_ref[...].astype(F32) * (w[0:1] * v2 + w[1:2] * v1 + w[2:3] * v)
    merged = jax.nn.sigmoid(g0_ref[...].astype(F32)) * _dot(ya_ref[...], wb_ref[0])
    merged += jax.nn.sigmoid(g1_ref[...].astype(F32)) * _dot(yb.astype(BF16), wb_ref[1])
    merged += jax.nn.sigmoid(g2_ref[...].astype(F32)) * _dot(yc_ref[...], wb_ref[2])
    o_ref[...] = merged.astype(BF16)


def _merge(ya, yc, proj, w_conv, w_branch, *, conv_col0, gate_col0, tm=256, halo=16):
    S = proj.shape[0]
    W = BRANCH_W
    D = w_branch.shape[-1]
    cblk = conv_col0 // W
    gblk = gate_col0 // D
    hb = tm // halo
    row = lambda b: pl.BlockSpec((tm, W), lambda i, b=b: (i, b))
    halo_spec = lambda b: pl.BlockSpec((halo, W), lambda i, b=b: (jnp.maximum(i * hb - 1, 0), b))
    gate = lambda b: pl.BlockSpec((tm, D), lambda i, b=b: (i, b))
    return pl.pallas_call(
        functools.partial(_merge_kernel, halo=halo),
        grid=(S // tm,),
        in_specs=[pl.BlockSpec((tm, W), lambda i: (i, 0)),
                  row(cblk), row(cblk + 1), row(cblk + 2), halo_spec(cblk + 1), halo_spec(cblk + 2),
                  pl.BlockSpec((CONV_K, W), lambda i: (0, 0)),
                  pl.BlockSpec((tm, W), lambda i: (i, 0)),
                  gate(gblk), gate(gblk + 1), gate(gblk + 2),
                  pl.BlockSpec((N_BRANCH, W, D), lambda i: (0, 0, 0))],
        out_specs=pl.BlockSpec((tm, D), lambda i: (i, 0)),
        out_shape=jax.ShapeDtypeStruct((S, D), BF16),
        compiler_params=_params("parallel"),
        name="conv_gated_merge",
    )(ya, proj, proj, proj, proj, proj, w_conv, yc, proj, proj, proj, w_branch)


def _proj_residual_kernel(a_ref, w_ref, g_ref, x_ref, o_ref):
    o_ref[...] = x_ref[...] + _rmsnorm(_dot(a_ref[...], w_ref[...]), g_ref[...])


def _proj_residual(a, w, g, x, *, tm=512):
    S, K = a.shape
    D = w.shape[1]
    return pl.pallas_call(
        _proj_residual_kernel,
        grid=(S // tm,),
        in_specs=[pl.BlockSpec((tm, K), lambda i: (i, 0)),
                  pl.BlockSpec((K, D), lambda i: (0, 0)),
                  pl.BlockSpec((1, D), lambda i: (0, 0)),
                  pl.BlockSpec((tm, D), lambda i: (i, 0))],
        out_specs=pl.BlockSpec((tm, D), lambda i: (i, 0)),
        out_shape=jax.ShapeDtypeStruct((S, D), F32),
        compiler_params=_params("parallel"),
        name="proj_norm_residual",
    )(a, w, g, x)


def _mem_kv_kernel(mem_ref, g_ref, w_ref, o_ref):
    o_ref[...] = _dot(_rmsnorm(mem_ref[...], g_ref[...]).astype(BF16), w_ref[...]).astype(BF16)


def _mem_kv(mem, g, wkv):
    M, D = mem.shape
    N = wkv.shape[1]
    return pl.pallas_call(
        _mem_kv_kernel,
        out_shape=jax.ShapeDtypeStruct((M, N), BF16),
        compiler_params=pltpu.CompilerParams(vmem_limit_bytes=VMEM_LIMIT),
        name="memory_kv",
    )(mem, g, wkv)


def _cross_attn_kernel(x_ref, gpre_ref, wq_ref, kv_ref, wo_ref, gpost_ref, o_ref):
    x = x_ref[...]
    q = _dot(_rmsnorm(x, gpre_ref[...]).astype(BF16), wq_ref[...]).astype(BF16)
    width = X_HEADS * X_HEAD_DIM
    outs = []
    for h in range(X_HEADS):
        cols = slice(h * X_HEAD_DIM, (h + 1) * X_HEAD_DIM)
        k = kv_ref[:, cols]
        v = kv_ref[:, width + h * X_HEAD_DIM: width + (h + 1) * X_HEAD_DIM]
        s = _dot_nt(q[:, cols], k)
        p = jnp.exp(s - jnp.max(s, axis=-1, keepdims=True))
        p = p / jnp.sum(p, axis=-1, keepdims=True)
        outs.append(_dot(p.astype(BF16), v).astype(BF16))
    o = jnp.concatenate(outs, axis=-1)
    o_ref[...] = x + _rmsnorm(_dot(o, wo_ref[...]), gpost_ref[...])


def _cross_attention(x, g_pre, wq, kv, wo, g_post, *, tm=512):
    S, D = x.shape
    full = lambda a: pl.BlockSpec(a.shape, lambda i: (0,) * a.ndim)
    return pl.pallas_call(
        _cross_attn_kernel,
        grid=(S // tm,),
        in_specs=[pl.BlockSpec((tm, D), lambda i: (i, 0)), full(g_pre), full(wq), full(kv), full(wo), full(g_post)],
        out_specs=pl.BlockSpec((tm, D), lambda i: (i, 0)),
        out_shape=jax.ShapeDtypeStruct((S, D), F32),
        compiler_params=_params("parallel"),
        name="cross_attention",
    )(x, g_pre, wq, kv, wo, g_post)


def _swiglu_partial(hn, wg, wu, wd):
    a = _dot(hn, wg)
    u = _dot(hn, wu)
    return _dot((a * jax.nn.sigmoid(a) * u).astype(BF16), wd)


def _ffn_kernel(x_ref, gpre_ref, wg_ref, wu_ref, wd_ref, gpost_ref, o_ref, hn_ref, acc_ref):
    f = pl.program_id(1)

    @pl.when(f == 0)
    def _():
        hn_ref[...] = _rmsnorm(x_ref[...], gpre_ref[...]).astype(BF16)
        acc_ref[...] = jnp.zeros_like(acc_ref)

    acc_ref[...] += _swiglu_partial(hn_ref[...], wg_ref[...], wu_ref[...], wd_ref[...])

    @pl.when(f == pl.num_programs(1) - 1)
    def _():
        o_ref[...] = x_ref[...] + _rmsnorm(acc_ref[...], gpost_ref[...])


def _ffn(x, g_pre, wg, wu, wd, g_post, *, tm=512, tf=512):
    S, D = x.shape
    F = wg.shape[1]
    return pl.pallas_call(
        _ffn_kernel,
        grid=(S // tm, F // tf),
        in_specs=[pl.BlockSpec((tm, D), lambda i, f: (i, 0)),
                  pl.BlockSpec((1, D), lambda i, f: (0, 0)),
                  pl.BlockSpec((D, tf), lambda i, f: (0, f)),
                  pl.BlockSpec((D, tf), lambda i, f: (0, f)),
                  pl.BlockSpec((tf, D), lambda i, f: (f, 0)),
                  pl.BlockSpec((1, D), lambda i, f: (0, 0))],
        out_specs=pl.BlockSpec((tm, D), lambda i, f: (i, 0)),
        out_shape=jax.ShapeDtypeStruct((S, D), F32),
        scratch_shapes=[pltpu.VMEM((tm, D), BF16), pltpu.VMEM((tm, D), F32)],
        compiler_params=_params("parallel", "arbitrary"),
        name="swiglu_ffn",
    )(x, g_pre, wg, wu, wd, g_post)


def _router_kernel(x_ref, g_ref, wr_ref, comb_ref):
    hn = _rmsnorm(x_ref[...], g_ref[...])
    logits = jnp.dot(hn, wr_ref[...], preferred_element_type=F32, precision=lax.Precision.HIGHEST)
    lane = lax.broadcasted_iota(jnp.int32, logits.shape, 1)
    neg_inf = -jnp.inf
    lg = jnp.where(lane < N_EXPERTS, logits, neg_inf)
    m1 = jnp.max(lg, axis=-1, keepdims=True)
    i1 = jnp.min(jnp.where(lg == m1, lane, LANES), axis=-1, keepdims=True)
    lg2 = jnp.where(lane == i1, neg_inf, lg)
    m2 = jnp.max(lg2, axis=-1, keepdims=True)
    i2 = jnp.min(jnp.where(lg2 == m2, lane, LANES), axis=-1, keepdims=True)
    e = jnp.exp(m2 - m1)
    g1 = 1.0 / (1.0 + e)
    comb_ref[...] = jnp.where(lane == i1, g1, 0.0) + jnp.where(lane == i2, e * g1, 0.0)


def _router(x, g_pre, w_router_padded, *, tm=512):
    S, D = x.shape
    return pl.pallas_call(
        _router_kernel,
        grid=(S // tm,),
        in_specs=[pl.BlockSpec((tm, D), lambda i: (i, 0)),
                  pl.BlockSpec((1, D), lambda i: (0, 0)),
                  pl.BlockSpec((D, LANES), lambda i: (0, 0))],
        out_specs=pl.BlockSpec((tm, LANES), lambda i: (i, 0)),
        out_shape=jax.ShapeDtypeStruct((S, LANES), F32),
        compiler_params=_params("parallel"),
        name="moe_router",
    )(x, g_pre, w_router_padded)


def _moe_kernel(x_ref, gpre_ref, comb_ref, wg_ref, wu_ref, wd_ref, gpost_ref, o_ref,
                hn_ref, acc_ref, eacc_ref):
    e = pl.program_id(1)
    f = pl.program_id(2)
    last_f = pl.num_programs(2) - 1

    @pl.when((e == 0) & (f == 0))
    def _():
        hn_ref[...] = _rmsnorm(x_ref[...], gpre_ref[...]).astype(BF16)
        acc_ref[...] = jnp.zeros_like(acc_ref)

    part = _swiglu_partial(hn_ref[...], wg_ref[...], wu_ref[...], wd_ref[...])

    @pl.when(f == 0)
    def _():
        eacc_ref[...] = part

    @pl.when(f > 0)
    def _():
        eacc_ref[...] += part

    @pl.when(f == last_f)
    def _():
        comb = comb_ref[...]
        lane = lax.broadcasted_iota(jnp.int32, comb.shape, 1)
        c = jnp.sum(jnp.where(lane == e, comb, 0.0), axis=-1, keepdims=True)
        acc_ref[...] += c * eacc_ref[...]

    @pl.when((e == pl.num_programs(1) - 1) & (f == last_f))
    def _():
        o_ref[...] = x_ref[...] + _rmsnorm(acc_ref[...], gpost_ref[...])


def _moe(x, g_pre, comb, wg, wu, wd, g_post, *, tm=512, tf=512):
    S, D = x.shape
    E, _, F = wg.shape
    return pl.pallas_call(
        _moe_kernel,
        grid=(S // tm, E, F // tf),
        in_specs=[pl.BlockSpec((tm, D), lambda i, e, f: (i, 0)),
                  pl.BlockSpec((1, D), lambda i, e, f: (0, 0)),
                  pl.BlockSpec((tm, LANES), lambda i, e, f: (i, 0)),
                  pl.BlockSpec((None, D, tf), lambda i, e, f: (e, 0, f)),
                  pl.BlockSpec((None, D, tf), lambda i, e, f: (e, 0, f)),
                  pl.BlockSpec((None, tf, D), lambda i, e, f: (e, f, 0)),
                  pl.BlockSpec((1, D), lambda i, e, f: (0, 0))],
        out_specs=pl.BlockSpec((tm, D), lambda i, e, f: (i, 0)),
        out_shape=jax.ShapeDtypeStruct((S, D), F32),
        scratch_shapes=[pltpu.VMEM((tm, D), BF16), pltpu.VMEM((tm, D), F32), pltpu.VMEM((tm, D), F32)],
        compiler_params=_params("parallel", "arbitrary", "arbitrary"),
        name="moe_swiglu",
    )(x, g_pre, comb, wg, wu, wd, g_post)


def _rope_tables(S):
    half = RET_D // 2
    inv = ROPE_BASE ** (-jnp.arange(half, dtype=F32) / half)
    ang = jnp.arange(S, dtype=jnp.int32)[:, None].astype(F32) * inv[None, :]
    cos, sin = jnp.cos(ang), jnp.sin(ang)
    return jnp.concatenate([cos, cos], axis=-1), jnp.concatenate([-sin, sin], axis=-1)


def _scaled_in_weights(w):
    ret_w = RET_HEADS * RET_D
    diff_col0 = 4 * ret_w + 3 * CONV_WIDTH
    diff_w = DIFF_HEADS * 2 * DIFF_D
    scale = jnp.ones((w.shape[1],), F32)
    scale = scale.at[:ret_w].set(RET_D ** -0.5)
    scale = scale.at[diff_col0:diff_col0 + diff_w].set(DIFF_D ** -0.5)
    return (w * scale[None, :]).astype(BF16)


def kernel(x, mem, t5_bias, w_in, w_conv, diff_lambda, diff_subln, w_branch, w_mix_out, w_xq, w_xkv, w_xo, w_ffn_gate, w_ffn_up, w_ffn_down, w_router, w_exp_gate, w_exp_up, w_exp_down, g_pre_mix, g_post_mix, g_pre_xattn, g_mem, g_post_xattn, g_pre_ffn, g_post_ffn):
    B, S, D = x.shape
    assert B == 1
    depth = w_in.shape[0]
    ret_w = RET_HEADS * RET_D
    conv_col0 = 4 * ret_w
    diff_col0 = conv_col0 + 3 * CONV_WIDTH
    gate_col0 = diff_col0 + 3 * DIFF_HEADS * 2 * DIFF_D
    cos2, sin2 = _rope_tables(S)
    xs = x[0]
    mem2 = mem[0]
    row = lambda g: g[None, :]
    for l in range(depth):
        proj = _in_projection(xs, row(g_pre_mix[l]), _scaled_in_weights(w_in[l]))
        ya = _retention(proj, cos2, sin2)
        lam_init = 0.8 - 0.6 * math.exp(-0.3 * l)
        yc = _diff_attention(proj, t5_bias, diff_lambda[l], row(diff_subln[l]), lam_init, col0=diff_col0)
        merged = _merge(ya, yc, proj, w_conv[l], w_branch[l].astype(BF16),
                        conv_col0=conv_col0, gate_col0=gate_col0)
        xs = _proj_residual(merged, w_mix_out[l].astype(BF16), row(g_post_mix[l]), xs)
        kv = _mem_kv(mem2, row(g_mem[l]), w_xkv[l].astype(BF16))
        xs = _cross_attention(xs, row(g_pre_xattn[l]), (w_xq[l] * X_HEAD_DIM ** -0.5).astype(BF16), kv,
                              w_xo[l].astype(BF16), row(g_post_xattn[l]))
        if l % 2 == 0:
            xs = _ffn(xs, row(g_pre_ffn[l]), w_ffn_gate[l // 2].astype(BF16), w_ffn_up[l // 2].astype(BF16),
                      w_ffn_down[l // 2].astype(BF16), row(g_post_ffn[l]))
        else:
            wr = jnp.pad(w_router[l // 2], ((0, 0), (0, LANES - N_EXPERTS)))
            comb = _router(xs, row(g_pre_ffn[l]), wr)
            xs = _moe(xs, row(g_pre_ffn[l]), comb, w_exp_gate[l // 2].astype(BF16), w_exp_up[l // 2].astype(BF16),
                      w_exp_down[l // 2].astype(BF16), row(g_post_ffn[l]))
    return xs[None]
```

```python
import functools
import math

import numpy as np
import jax
import jax.numpy as jnp
from jax import lax
from jax.experimental import pallas as pl
from jax.experimental.pallas import tpu as pltpu

F32 = jnp.float32
BF16 = jnp.bfloat16

EPS = 1e-6
RET_HEADS = 8
RET_D = 128
RET_CHUNK = 128
ROPE_BASE = 10000.0
CONV_WIDTH = 1024
CONV_K = 3
DIFF_HEADS = 4
DIFF_D = 128
T5_BUCKETS = 32
T5_MAX_DIST = 128
X_HEADS = 4
X_HEAD_DIM = 128
N_EXPERTS = 8
N_BRANCH = 3
BRANCH_W = 1024

V7X_VMEM_BYTES = 64 * 1024 * 1024
VMEM_LIMIT = V7X_VMEM_BYTES - 8 * 1024 * 1024
LANES = 128
NEG_BIG = -1e30
LOG2E = math.log2(math.e)
ATTN_BLOCK = 512
FAR_UNROLL = 2
MOE_TILE = 512


def _params(*semantics):
    return pltpu.CompilerParams(dimension_semantics=semantics, vmem_limit_bytes=VMEM_LIMIT)


def _rmsnorm(x, g=None):
    y = x * lax.rsqrt(jnp.mean(x * x, axis=-1, keepdims=True) + EPS)
    return y if g is None else y * g


def _dot(a, b):
    return jnp.dot(a, b, preferred_element_type=F32)


def _dot_nt(a, b):
    return lax.dot_general(a, b, (((1,), (1,)), ((), ())), preferred_element_type=F32)


def _in_proj_kernel(x_ref, g_ref, w_ref, o_ref, hn_ref):
    @pl.when(pl.program_id(1) == 0)
    def _():
        hn_ref[...] = _rmsnorm(x_ref[...], g_ref[...]).astype(BF16)

    o_ref[...] = _dot(hn_ref[...], w_ref[...]).astype(BF16)


def _in_projection(x, g, w, *, tm=1024, tn=1024):
    S, D = x.shape
    N = w.shape[1]
    return pl.pallas_call(
        _in_proj_kernel,
        grid=(S // tm, N // tn),
        in_specs=[pl.BlockSpec((tm, D), lambda i, j: (i, 0)),
                  pl.BlockSpec((1, D), lambda i, j: (0, 0)),
                  pl.BlockSpec((D, tn), lambda i, j: (0, j))],
        out_specs=pl.BlockSpec((tm, tn), lambda i, j: (i, j)),
        out_shape=jax.ShapeDtypeStruct((S, N), BF16),
        scratch_shapes=[pltpu.VMEM((tm, D), BF16)],
        compiler_params=_params("parallel", "arbitrary"),
        name="in_projection",
    )(x, g, w)


def _retention_consts():
    C, H = RET_CHUNK, RET_HEADS
    log_g = np.log1p(-(2.0 ** (-5.0 - np.arange(H, dtype=np.float64))))
    idx = np.arange(C, dtype=np.float64)
    dist = idx[:, None] - idx[None, :]
    intra = np.where(dist[None] >= 0, np.exp(np.maximum(dist, 0.0)[None] * log_g[:, None, None]), 0.0)
    kdec = np.exp((C - 1 - idx)[None, :, None] * log_g[:, None, None]) * np.ones((1, 1, RET_D))
    qdec = np.exp((idx + 1.0)[None, :, None] * log_g[:, None, None]) * np.ones((1, 1, RET_D))
    chunk_decay = tuple(float(v) for v in np.exp(C * log_g))
    return (jnp.asarray(intra, F32), jnp.asarray(qdec, F32), jnp.asarray(kdec, F32), chunk_decay)


def _rope(x, cos2, sin2):
    return x * cos2 + pltpu.roll(x, RET_D // 2, 1) * sin2


def _retention_kernel(q_ref, k_ref, v_ref, g_ref, cos_ref, sin_ref, intra_ref, qdec_ref, kdec_ref,
                      o_ref, state_ref, *, chunks, chunk_decay):
    @pl.when(pl.program_id(0) == 0)
    def _():
        state_ref[...] = jnp.zeros_like(state_ref)

    C = RET_CHUNK
    for c in range(chunks):
        rows = slice(c * C, (c + 1) * C)
        cos2 = cos_ref[rows, :]
        sin2 = sin_ref[rows, :]
        for h in range(RET_HEADS):
            cols = slice(h * RET_D, (h + 1) * RET_D)
            q = _rope(q_ref[rows, cols].astype(F32), cos2, sin2)
            k = _rope(k_ref[rows, cols].astype(F32), cos2, sin2)
            v = v_ref[rows, cols]
            state = state_ref[h]
            scores = _dot_nt(q.astype(BF16), k.astype(BF16)) * intra_ref[h]
            o = _dot(scores.astype(BF16), v) + _dot((q * qdec_ref[h]).astype(BF16), state.astype(BF16))
            kv = _dot((k * kdec_ref[h]).T.astype(BF16), v)
            state_ref[h] = state * chunk_decay[h] + kv
            gate = g_ref[rows, cols].astype(F32)
            o_ref[rows, cols] = (gate * jax.nn.sigmoid(gate) * _rmsnorm(o)).astype(BF16)


def _retention(proj, cos2, sin2, *, chunks=2):
    S = proj.shape[0]
    W = RET_HEADS * RET_D
    R = chunks * RET_CHUNK
    intra, qdec, kdec, chunk_decay = _retention_consts()
    col = lambda b: pl.BlockSpec((R, W), lambda i, b=b: (i, b))
    const = pl.BlockSpec((RET_HEADS, RET_CHUNK, RET_D), lambda i: (0, 0, 0))
    rope_spec = pl.BlockSpec((R, RET_D), lambda i: (i, 0))
    return pl.pallas_call(
        functools.partial(_retention_kernel, chunks=chunks, chunk_decay=chunk_decay),
        grid=(S // R,),
        in_specs=[col(0), col(1), col(2), col(3), rope_spec, rope_spec, const, const, const],
        out_specs=pl.BlockSpec((R, W), lambda i: (i, 0)),
        out_shape=jax.ShapeDtypeStruct((S, W), BF16),
        scratch_shapes=[pltpu.VMEM((RET_HEADS, RET_D, RET_D), F32)],
        compiler_params=_params("arbitrary"),
        name="retention",
    )(proj, proj, proj, proj, cos2, sin2, intra, qdec, kdec)


def _t5_bucket(dist):
    max_exact = T5_BUCKETS // 2
    d = jnp.maximum(dist, 0)
    df = jnp.maximum(d, 1).astype(F32)
    large = max_exact + (jnp.log(df / max_exact) / math.log(T5_MAX_DIST / max_exact)
                         * (T5_BUCKETS - max_exact)).astype(jnp.int32)
    large = jnp.minimum(large, T5_BUCKETS - 1)
    return jnp.where(d < max_exact, d, large)


def _near_bias_tiles(t5_bias, t):
    assert t >= T5_MAX_DIST
    ql = jnp.arange(t, dtype=jnp.int32)[:, None]
    kl = jnp.arange(t, dtype=jnp.int32)[None, :]
    tab = t5_bias.T.astype(F32) * LOG2E
    tiles = []
    for off in (0, 1):
        dist = ql + off * t - kl
        bucket = _t5_bucket(dist)[None]
        b = jnp.zeros((tab.shape[0], t, t), F32)
        for n in range(T5_BUCKETS):
            b = jnp.where(bucket == n, tab[:, n, None, None], b)
        tiles.append(jnp.where(dist[None] >= 0, b, NEG_BIG))
    return jnp.stack(tiles, axis=1)


def _diff_attn_kernel(t5_ref, lam_ref, q_ref, k_ref, v_ref, bias_ref, subg_ref, o_ref,
                      mx_ref, l_ref, acc_ref, *, t, lam_init):
    h = pl.program_id(0)
    i = pl.program_id(1)
    far_bias = t5_ref[h, T5_BUCKETS - 1] * LOG2E
    q = q_ref[...]

    def logits(mp, k, bias):
        cols = slice(mp * DIFF_D, (mp + 1) * DIFF_D)
        s = _dot_nt(q[:, cols], k[:, cols])
        return s if bias is None else s + bias

    def lane_groups(s):
        return [s[:, g * LANES:(g + 1) * LANES] for g in range(t // LANES)]

    def key_block(j):
        return pl.ds(pl.multiple_of(j * t, t), t)

    def max_block(j, bias):
        k = k_ref[key_block(j), :]
        for mp in range(2):
            mx = mx_ref[mp]
            for sg in lane_groups(logits(mp, k, bias)):
                mx = jnp.maximum(mx, sg)
            mx_ref[mp] = mx

    def acc_block(j, bias, shift):
        k = k_ref[key_block(j), :]
        v = v_ref[key_block(j), :]
        for mp in range(2):
            m = mx_ref[mp] - shift
            ps = [jnp.exp2(sg - m) for sg in lane_groups(logits(mp, k, bias))]
            l_ref[mp] += functools.reduce(lambda a, b: a + b, ps)
            acc_ref[mp] += _dot(jnp.concatenate(ps, axis=-1).astype(BF16), v)

    def near_blocks(fn, *args):
        @pl.when(i >= 1)
        def _():
            fn(i - 1, bias_ref[1], *args)

        fn(i, bias_ref[0], *args)

    def far_blocks(fn, *args):
        n = jnp.maximum(i - 1, 0)

        def body(jj, carry):
            for u in range(FAR_UNROLL):
                fn(jj * FAR_UNROLL + u, None, *args)
            return carry

        lax.fori_loop(0, n // FAR_UNROLL, body, 0)
        for r in range(1, FAR_UNROLL):
            @pl.when(n % FAR_UNROLL >= r)
            def _():
                fn(n - (n % FAR_UNROLL) + r - 1, None, *args)

    mx_ref[...] = jnp.full_like(mx_ref, NEG_BIG)
    far_blocks(max_block)
    mx_ref[...] = mx_ref[...] + far_bias
    near_blocks(max_block)
    for mp in range(2):
        mx_ref[mp] = jnp.broadcast_to(jnp.max(mx_ref[mp], axis=-1, keepdims=True), (t, LANES))

    l_ref[...] = jnp.zeros_like(l_ref)
    acc_ref[...] = jnp.zeros_like(acc_ref)
    far_blocks(acc_block, far_bias)
    near_blocks(acc_block, 0.0)

    lam_p = lam_ref[...]
    lam = (jnp.exp(jnp.sum(lam_p[0:1] * lam_p[1:2], axis=-1, keepdims=True))
           - jnp.exp(jnp.sum(lam_p[2:3] * lam_p[3:4], axis=-1, keepdims=True)) + lam_init)
    l0 = jnp.sum(l_ref[0], axis=-1, keepdims=True)
    l1 = jnp.sum(l_ref[1], axis=-1, keepdims=True)
    o = acc_ref[0] / l0 - lam * (acc_ref[1] / l1)
    o_ref[...] = (_rmsnorm(o, subg_ref[...]) * (1.0 - lam_init)).astype(BF16)


def _diff_attention(proj, bias, t5_bias, lam_params, subln_g, lam_init, *, col0, t):
    S = proj.shape[0]
    HW = 2 * DIFF_D
    qb = col0 // HW
    kb = qb + DIFF_HEADS
    vb = kb + DIFF_HEADS
    return pl.pallas_call(
        functools.partial(_diff_attn_kernel, t=t, lam_init=lam_init),
        grid=(DIFF_HEADS, S // t),
        in_specs=[pl.BlockSpec(memory_space=pltpu.SMEM),
                  pl.BlockSpec((4, DIFF_D), lambda h, i: (0, 0)),
                  pl.BlockSpec((t, HW), lambda h, i: (i, qb + h)),
                  pl.BlockSpec((S, HW), lambda h, i: (0, kb + h)),
                  pl.BlockSpec((S, HW), lambda h, i: (0, vb + h)),
                  pl.BlockSpec((None, 2, t, t), lambda h, i: (h, 0, 0, 0)),
                  pl.BlockSpec((1, HW), lambda h, i: (0, 0))],
        out_specs=pl.BlockSpec((t, HW), lambda h, i: (i, h)),
        out_shape=jax.ShapeDtypeStruct((S, DIFF_HEADS * HW), BF16),
        scratch_shapes=[pltpu.VMEM((2, t, LANES), F32), pltpu.VMEM((2, t, LANES), F32),
                        pltpu.VMEM((2, t, HW), F32)],
        compiler_params=_params("arbitrary", "arbitrary"),
        name="diff_attention",
    )(t5_bias.T.astype(F32), lam_params, proj, proj, proj, bias, subln_g)


def _merge_kernel(ya_ref, cb_ref, cc_ref, cu_ref, hc_ref, hu_ref, wconv_ref, yc_ref,
                  g0_ref, g1_ref, g2_ref, wb_ref, o_ref, *, halo):
    i = pl.program_id(0)
    v = cc_ref[...].astype(F32) * cu_ref[...].astype(F32)
    hv = jnp.where(i > 0, hc_ref[...].astype(F32) * hu_ref[...].astype(F32), 0.0)
    prev1 = hv[halo - 1:halo, :]
    prev2 = hv[halo - 2:halo - 1, :]
    row = lax.broadcasted_iota(jnp.int32, v.shape, 0)
    v1 = jnp.where(row == 0, prev1, pltpu.roll(v, 1, 0))
    v2 = jnp.where(row == 0, prev2, jnp.where(row == 1, prev1, pltpu.roll(v, 2, 0)))
    w = wconv_ref[...]
    yb = cb_ref[...].astype(F32) * (w[0:1] * v2 + w[1:2] * v1 + w[2:3] * v)
    merged = jax.nn.sigmoid(g0_ref[...].astype(F32)) * _dot(ya_ref[...], wb_ref[0])
    merged += jax.nn.sigmoid(g1_ref[...].astype(F32)) * _dot(yb.astype(BF16), wb_ref[1])
    merged += jax.nn.sigmoid(g2_ref[...].astype(F32)) * _dot(yc_ref[...], wb_ref[2])
    o_ref[...] = merged.astype(BF16)


def _merge(ya, yc, proj, w_conv, w_branch, *, conv_col0, gate_col0, tm=256, halo=16):
    S = proj.shape[0]
    W = BRANCH_W
    D = w_branch.shape[-1]
    cblk = conv_col0 // W
    gblk = gate_col0 // D
    hb = tm // halo
    row = lambda b: pl.BlockSpec((tm, W), lambda i, b=b: (i, b))
    halo_spec = lambda b: pl.BlockSpec((halo, W), lambda i, b=b: (jnp.maximum(i * hb - 1, 0), b))
    gate = lambda b: pl.BlockSpec((tm, D), lambda i, b=b: (i, b))
    return pl.pallas_call(
        functools.partial(_merge_kernel, halo=halo),
        grid=(S // tm,),
        in_specs=[pl.BlockSpec((tm, W), lambda i: (i, 0)),
                  row(cblk), row(cblk + 1), row(cblk + 2), halo_spec(cblk + 1), halo_spec(cblk + 2),
                  pl.BlockSpec((CONV_K, W), lambda i: (0, 0)),
                  pl.BlockSpec((tm, W), lambda i: (i, 0)),
                  gate(gblk), gate(gblk + 1), gate(gblk + 2),
                  pl.BlockSpec((N_BRANCH, W, D), lambda i: (0, 0, 0))],
        out_specs=pl.BlockSpec((tm, D), lambda i: (i, 0)),
        out_shape=jax.ShapeDtypeStruct((S, D), BF16),
        compiler_params=_params("parallel"),
        name="conv_gated_merge",
    )(ya, proj, proj, proj, proj, proj, w_conv, yc, proj, proj, proj, w_branch)


def _proj_residual_kernel(a_ref, w_ref, g_ref, x_ref, o_ref):
    o_ref[...] = x_ref[...] + _rmsnorm(_dot(a_ref[...], w_ref[...]), g_ref[...])


def _proj_residual(a, w, g, x, *, tm=512):
    S, K = a.shape
    D = w.shape[1]
    return pl.pallas_call(
        _proj_residual_kernel,
        grid=(S // tm,),
        in_specs=[pl.BlockSpec((tm, K), lambda i: (i, 0)),
                  pl.BlockSpec((K, D), lambda i: (0, 0)),
                  pl.BlockSpec((1, D), lambda i: (0, 0)),
                  pl.BlockSpec((tm, D), lambda i: (i, 0))],
        out_specs=pl.BlockSpec((tm, D), lambda i: (i, 0)),
        out_shape=jax.ShapeDtypeStruct((S, D), F32),
        compiler_params=_params("parallel"),
        name="proj_norm_residual",
    )(a, w, g, x)


def _mem_kv_kernel(mem_ref, g_ref, w_ref, o_ref):
    o_ref[...] = _dot(_rmsnorm(mem_ref[...], g_ref[...]).astype(BF16), w_ref[...]).astype(BF16)


def _mem_kv(mem, g, wkv):
    M, D = mem.shape
    N = wkv.shape[1]
    return pl.pallas_call(
        _mem_kv_kernel,
        out_shape=jax.ShapeDtypeStruct((M, N), BF16),
        compiler_params=pltpu.CompilerParams(vmem_limit_bytes=VMEM_LIMIT),
        name="memory_kv",
    )(mem, g, wkv)


def _cross_attn_kernel(x_ref, gpre_ref, wq_ref, kv_ref, wo_ref, gpost_ref, o_ref):
    x = x_ref[...]
    q = _dot(_rmsnorm(x, gpre_ref[...]).astype(BF16), wq_ref[...]).astype(BF16)
    width = X_HEADS * X_HEAD_DIM
    outs = []
    for h in range(X_HEADS):
        cols = slice(h * X_HEAD_DIM, (h + 1) * X_HEAD_DIM)
        k = kv_ref[:, cols]
        v = kv_ref[:, width + h * X_HEAD_DIM: width + (h + 1) * X_HEAD_DIM]
        s = _dot_nt(q[:, cols], k)
        p = jnp.exp(s - jnp.max(s, axis=-1, keepdims=True))
        p = p / jnp.sum(p, axis=-1, keepdims=True)
        outs.append(_dot(p.astype(BF16), v).astype(BF16))
    o = jnp.concatenate(outs, axis=-1)
    o_ref[...] = x + _rmsnorm(_dot(o, wo_ref[...]), gpost_ref[...])


def _cross_attention(x, g_pre, wq, kv, wo, g_post, *, tm=512):
    S, D = x.shape
    full = lambda a: pl.BlockSpec(a.shape, lambda i: (0,) * a.ndim)
    return pl.pallas_call(
        _cross_attn_kernel,
        grid=(S // tm,),
        in_specs=[pl.BlockSpec((tm, D), lambda i: (i, 0)), full(g_pre), full(wq), full(kv), full(wo), full(g_post)],
        out_specs=pl.BlockSpec((tm, D), lambda i: (i, 0)),
        out_shape=jax.ShapeDtypeStruct((S, D), F32),
        compiler_params=_params("parallel"),
        name="cross_attention",
    )(x, g_pre, wq, kv, wo, g_post)


def _swiglu_partial(hn, wg, wu, wd):
    a = _dot(hn, wg)
    u = _dot(hn, wu)
    return _dot((a * jax.nn.sigmoid(a) * u).astype(BF16), wd)


def _ffn_kernel(x_ref, gpre_ref, wg_ref, wu_ref, wd_ref, gpost_ref, o_ref, hn_ref, acc_ref):
    f = pl.program_id(1)

    @pl.when(f == 0)
    def _():
        hn_ref[...] = _rmsnorm(x_ref[...], gpre_ref[...]).astype(BF16)
        acc_ref[...] = jnp.zeros_like(acc_ref)

    acc_ref[...] += _swiglu_partial(hn_ref[...], wg_ref[...], wu_ref[...], wd_ref[...])

    @pl.when(f == pl.num_programs(1) - 1)
    def _():
        o_ref[...] = x_ref[...] + _rmsnorm(acc_ref[...], gpost_ref[...])


def _ffn(x, g_pre, wg, wu, wd, g_post, *, tm=512, tf=512):
    S, D = x.shape
    F = wg.shape[1]
    return pl.pallas_call(
        _ffn_kernel,
        grid=(S // tm, F // tf),
        in_specs=[pl.BlockSpec((tm, D), lambda i, f: (i, 0)),
                  pl.BlockSpec((1, D), lambda i, f: (0, 0)),
                  pl.BlockSpec((D, tf), lambda i, f: (0, f)),
                  pl.BlockSpec((D, tf), lambda i, f: (0, f)),
                  pl.BlockSpec((tf, D), lambda i, f: (f, 0)),
                  pl.BlockSpec((1, D), lambda i, f: (0, 0))],
        out_specs=pl.BlockSpec((tm, D), lambda i, f: (i, 0)),
        out_shape=jax.ShapeDtypeStruct((S, D), F32),
        scratch_shapes=[pltpu.VMEM((tm, D), BF16), pltpu.VMEM((tm, D), F32)],
        compiler_params=_params("parallel", "arbitrary"),
        name="swiglu_ffn",
    )(x, g_pre, wg, wu, wd, g_post)


def _router_kernel(x_ref, g_ref, wr_ref, hn_ref, sel_ref, gate_ref):
    hn = _rmsnorm(x_ref[...], g_ref[...])
    hn_ref[...] = hn
    logits = jnp.dot(hn, wr_ref[...], preferred_element_type=F32, precision=lax.Precision.HIGHEST)
    lane = lax.broadcasted_iota(jnp.int32, logits.shape, 1)
    neg_inf = -jnp.inf
    lg = jnp.where(lane < N_EXPERTS, logits, neg_inf)
    m1 = jnp.max(lg, axis=-1, keepdims=True)
    i1 = jnp.min(jnp.where(lg == m1, lane, LANES), axis=-1, keepdims=True)
    lg2 = jnp.where(lane == i1, neg_inf, lg)
    m2 = jnp.max(lg2, axis=-1, keepdims=True)
    i2 = jnp.min(jnp.where(lg2 == m2, lane, LANES), axis=-1, keepdims=True)
    e = jnp.exp(m2 - m1)
    g1 = 1.0 / (1.0 + e)
    sel_ref[...] = jnp.where(lane == 0, i1, jnp.where(lane == 1, i2, 0))
    gate_ref[...] = jnp.where(lane == 0, g1, jnp.where(lane == 1, e * g1, 0.0))


def _router(x, g_pre, w_router_padded, *, tm=512):
    S, D = x.shape
    return pl.pallas_call(
        _router_kernel,
        grid=(S // tm,),
        in_specs=[pl.BlockSpec((tm, D), lambda i: (i, 0)),
                  pl.BlockSpec((1, D), lambda i: (0, 0)),
                  pl.BlockSpec((D, LANES), lambda i: (0, 0))],
        out_specs=[pl.BlockSpec((tm, D), lambda i: (i, 0)),
                   pl.BlockSpec((tm, LANES), lambda i: (i, 0)),
                   pl.BlockSpec((tm, LANES), lambda i: (i, 0))],
        out_shape=[jax.ShapeDtypeStruct((S, D), F32),
                   jax.ShapeDtypeStruct((S, LANES), jnp.int32),
                   jax.ShapeDtypeStruct((S, LANES), F32)],
        compiler_params=_params("parallel"),
        name="moe_router",
    )(x, g_pre, w_router_padded)


def _moe_plan(sel, gate, tg):
    S = sel.shape[0]
    P, E = 2 * S, N_EXPERTS
    i32 = jnp.int32
    e_flat = sel[:, :2].reshape(P)
    g_flat = gate[:, :2].reshape(P)
    onehot = (e_flat[:, None] == jnp.arange(E, dtype=i32)[None, :]).astype(i32)
    counts = jnp.sum(onehot, axis=0)
    rank = jnp.sum((jnp.cumsum(onehot, axis=0) - 1) * onehot, axis=1)
    ustart = jnp.cumsum(counts) - counts
    tiles = (counts + tg - 1) // tg
    tile_end = jnp.cumsum(tiles)
    tstart = tile_end - tiles
    n_tiles = P // tg + E
    n = jnp.arange(n_tiles, dtype=i32)
    tile_e = jnp.minimum(jnp.sum((n[:, None] >= tile_end[None, :]).astype(i32), axis=1), E - 1)
    used = n < tile_end[-1]
    local = ((n - tstart[tile_e]) * tg)[:, None] + jnp.arange(tg, dtype=i32)[None, :]
    valid = used[:, None] & (local < counts[tile_e][:, None])
    order = jnp.argsort(e_flat, stable=True).astype(i32)
    pair = order[jnp.clip(ustart[tile_e][:, None] + local, 0, P - 1)]
    src_token = jnp.where(valid, pair // 2, 0).reshape(-1)
    row_gate = jnp.where(valid, g_flat[pair], 0.0).reshape(-1, 1)
    dest = tstart[e_flat] * tg + rank
    return tile_e, used.astype(i32), src_token, row_gate, dest


def _row_gather(idx_ref, base, src_hbm, dst_ref, sem, n_rows, *, wait):
    def body(r, carry):
        cp = pltpu.make_async_copy(src_hbm.at[pl.ds(idx_ref[base + r], 1), :], dst_ref.at[pl.ds(r, 1), :], sem)
        if wait:
            cp.wait()
        else:
            cp.start()
        return carry

    lax.fori_loop(0, n_rows, body, 0, unroll=8)


def _moe_group_kernel(te_ref, used_ref, src_ref, hn_hbm, gate_ref, wg_ref, wu_ref, wd_ref, y_ref,
                      xg_ref, hb_ref, acc_ref, sem, *, tg):
    n = pl.program_id(0)
    f = pl.program_id(1)
    slot = n % 2

    @pl.when(f == 0)
    def _():
        @pl.when(n == 0)
        def _():
            _row_gather(src_ref, 0, hn_hbm, xg_ref.at[0], sem.at[0], tg, wait=False)

        @pl.when(n + 1 < pl.num_programs(0))
        def _():
            _row_gather(src_ref, (n + 1) * tg, hn_hbm, xg_ref.at[1 - slot], sem.at[1 - slot], tg, wait=False)

        _row_gather(src_ref, n * tg, hn_hbm, xg_ref.at[slot], sem.at[slot], tg, wait=True)
        hb_ref[...] = xg_ref[slot].astype(BF16)

    @pl.when(used_ref[n] > 0)
    def _():
        part = _swiglu_partial(hb_ref[...], wg_ref[...], wu_ref[...], wd_ref[...])

        @pl.when(f == 0)
        def _():
            acc_ref[...] = part

        @pl.when(f > 0)
        def _():
            acc_ref[...] += part

    @pl.when(f == pl.num_programs(1) - 1)
    def _():
        y_ref[...] = jnp.where(used_ref[n] > 0, acc_ref[...] * gate_ref[...], 0.0)


def _moe_group(hn, tile_e, used, src_token, row_gate, wg, wu, wd, *, tg, tf=512):
    S, D = hn.shape
    E, _, F = wg.shape
    n_tiles = tile_e.shape[0]
    nf = F // tf
    fblk = lambda n, f, used: jnp.where(used[n] > 0, f, nf - 1)
    grid_spec = pltpu.PrefetchScalarGridSpec(
        num_scalar_prefetch=3,
        grid=(n_tiles, nf),
        in_specs=[pl.BlockSpec(memory_space=pl.ANY),
                  pl.BlockSpec((tg, 1), lambda n, f, te, used, src: (n, 0)),
                  pl.BlockSpec((None, D, tf), lambda n, f, te, used, src: (te[n], 0, fblk(n, f, used))),
                  pl.BlockSpec((None, D, tf), lambda n, f, te, used, src: (te[n], 0, fblk(n, f, used))),
                  pl.BlockSpec((None, tf, D), lambda n, f, te, used, src: (te[n], fblk(n, f, used), 0))],
        out_specs=pl.BlockSpec((tg, D), lambda n, f, te, used, src: (n, 0)),
        scratch_shapes=[pltpu.VMEM((2, tg, D), F32), pltpu.VMEM((tg, D), BF16), pltpu.VMEM((tg, D), F32),
                        pltpu.SemaphoreType.DMA((2,))],
    )
    return pl.pallas_call(
        functools.partial(_moe_group_kernel, tg=tg),
        grid_spec=grid_spec,
        out_shape=jax.ShapeDtypeStruct((n_tiles * tg, D), F32),
        compiler_params=_params("arbitrary", "arbitrary"),
        name="moe_grouped_swiglu",
    )(tile_e, used, src_token, hn, row_gate, wg, wu, wd)


def _moe_combine_kernel(dest_ref, x_ref, y_hbm, g_ref, o_ref, yb_ref, sem, *, tm):
    i = pl.program_id(0)
    slot = i % 2

    def gather(tile, s, wait):
        def body(r, carry):
            for k in range(2):
                row = dest_ref[2 * (tile * tm + r) + k]
                cp = pltpu.make_async_copy(y_hbm.at[pl.ds(row, 1), :], yb_ref.at[s, k, pl.ds(r, 1), :], sem.at[s])
                if wait:
                    cp.wait()
                else:
                    cp.start()
            return carry

        lax.fori_loop(0, tm, body, 0, unroll=4)

    @pl.when(i == 0)
    def _():
        gather(0, 0, False)

    @pl.when(i + 1 < pl.num_programs(0))
    def _():
        gather(i + 1, 1 - slot, False)

    gather(i, slot, True)
    o_ref[...] = x_ref[...] + _rmsnorm(yb_ref[slot, 0] + yb_ref[slot, 1], g_ref[...])


def _moe_combine(x, y, dest, g_post, *, tm=256):
    S, D = x.shape
    grid_spec = pltpu.PrefetchScalarGridSpec(
        num_scalar_prefetch=1,
        grid=(S // tm,),
        in_specs=[pl.BlockSpec((tm, D), lambda i, dest: (i, 0)),
                  pl.BlockSpec(memory_space=pl.ANY),
                  pl.BlockSpec((1, D), lambda i, dest: (0, 0))],
        out_specs=pl.BlockSpec((tm, D), lambda i, dest: (i, 0)),
        scratch_shapes=[pltpu.VMEM((2, 2, tm, D), F32), pltpu.SemaphoreType.DMA((2,))],
    )
    return pl.pallas_call(
        functools.partial(_moe_combine_kernel, tm=tm),
        grid_spec=grid_spec,
        out_shape=jax.ShapeDtypeStruct((S, D), F32),
        compiler_params=_params("arbitrary"),
        name="moe_combine",
    )(dest, x, y, g_post)


def _rope_tables(S):
    half = RET_D // 2
    inv = ROPE_BASE ** (-jnp.arange(half, dtype=F32) / half)
    ang = jnp.arange(S, dtype=jnp.int32)[:, None].astype(F32) * inv[None, :]
    cos, sin = jnp.cos(ang), jnp.sin(ang)
    return jnp.concatenate([cos, cos], axis=-1), jnp.concatenate([-sin, sin], axis=-1)


def _scaled_in_weights(w):
    ret_w = RET_HEADS * RET_D
    diff_col0 = 4 * ret_w + 3 * CONV_WIDTH
    diff_w = DIFF_HEADS * 2 * DIFF_D
    scale = jnp.ones((w.shape[1],), F32)
    scale = scale.at[:ret_w].set(RET_D ** -0.5)
    scale = scale.at[diff_col0:diff_col0 + diff_w].set(DIFF_D ** -0.5 * LOG2E)
    return (w * scale[None, :]).astype(BF16)


def kernel(x, mem, t5_bias, w_in, w_conv, diff_lambda, diff_subln, w_branch, w_mix_out, w_xq, w_xkv, w_xo, w_ffn_gate, w_ffn_up, w_ffn_down, w_router, w_exp_gate, w_exp_up, w_exp_down, g_pre_mix, g_post_mix, g_pre_xattn, g_mem, g_post_xattn, g_pre_ffn, g_post_ffn):
    B, S, D = x.shape
    assert B == 1
    depth = w_in.shape[0]
    ret_w = RET_HEADS * RET_D
    conv_col0 = 4 * ret_w
    diff_col0 = conv_col0 + 3 * CONV_WIDTH
    gate_col0 = diff_col0 + 3 * DIFF_HEADS * 2 * DIFF_D
    cos2, sin2 = _rope_tables(S)
    near_bias = _near_bias_tiles(t5_bias, ATTN_BLOCK)
    xs = x[0]
    mem2 = mem[0]
    row = lambda g: g[None, :]
    for l in range(depth):
        proj = _in_projection(xs, row(g_pre_mix[l]), _scaled_in_weights(w_in[l]))
        ya = _retention(proj, cos2, sin2)
        lam_init = 0.8 - 0.6 * math.exp(-0.3 * l)
        yc = _diff_attention(proj, near_bias, t5_bias, diff_lambda[l], row(diff_subln[l]), lam_init,
                             col0=diff_col0, t=ATTN_BLOCK)
        merged = _merge(ya, yc, proj, w_conv[l], w_branch[l].astype(BF16),
                        conv_col0=conv_col0, gate_col0=gate_col0)
        xs = _proj_residual(merged, w_mix_out[l].astype(BF16), row(g_post_mix[l]), xs)
        kv = _mem_kv(mem2, row(g_mem[l]), w_xkv[l].astype(BF16))
        xs = _cross_attention(xs, row(g_pre_xattn[l]), (w_xq[l] * X_HEAD_DIM ** -0.5).astype(BF16), kv,
                              w_xo[l].astype(BF16), row(g_post_xattn[l]))
        if l % 2 == 0:
            xs = _ffn(xs, row(g_pre_ffn[l]), w_ffn_gate[l // 2].astype(BF16), w_ffn_up[l // 2].astype(BF16),
                      w_ffn_down[l // 2].astype(BF16), row(g_post_ffn[l]))
        else:
            wr = jnp.pad(w_router[l // 2], ((0, 0), (0, LANES - N_EXPERTS)))
            hn, sel, gate = _router(xs, row(g_pre_ffn[l]), wr)
            tile_e, used, src_token, row_gate, dest = _moe_plan(sel, gate, MOE_TILE)
            y = _moe_group(hn, tile_e, used, src_token, row_gate, w_exp_gate[l // 2].astype(BF16),
                           w_exp_up[l // 2].astype(BF16), w_exp_down[l // 2].astype(BF16), tg=MOE_TILE)
            xs = _moe_combine(xs, y, dest, row(g_post_ffn[l]))
    return xs[None]
```

```python
import functools
import math

import numpy as np
import jax
import jax.numpy as jnp
from jax import lax
from jax.experimental import pallas as pl
from jax.experimental.pallas import tpu as pltpu

F32 = jnp.float32
BF16 = jnp.bfloat16

EPS = 1e-6
RET_HEADS = 8
RET_D = 128
RET_CHUNK = 128
ROPE_BASE = 10000.0
CONV_WIDTH = 1024
CONV_K = 3
DIFF_HEADS = 4
DIFF_D = 128
T5_BUCKETS = 32
T5_MAX_DIST = 128
X_HEADS = 4
X_HEAD_DIM = 128
N_EXPERTS = 8
N_BRANCH = 3
BRANCH_W = 1024

V7X_VMEM_BYTES = 64 * 1024 * 1024
VMEM_LIMIT = V7X_VMEM_BYTES - 8 * 1024 * 1024
LANES = 128
NEG_BIG = -1e30
LOG2E = math.log2(math.e)
ATTN_BLOCK = 512
FAR_UNROLL = 4
MOE_TILE = 512
FFN_TF = 1024


def _params(*semantics):
    return pltpu.CompilerParams(dimension_semantics=semantics, vmem_limit_bytes=VMEM_LIMIT)


def _rmsnorm(x, g=None):
    y = x * lax.rsqrt(jnp.mean(x * x, axis=-1, keepdims=True) + EPS)
    return y if g is None else y * g


def _dot(a, b):
    return jnp.dot(a, b, preferred_element_type=F32)


def _dot_nt(a, b):
    return lax.dot_general(a, b, (((1,), (1,)), ((), ())), preferred_element_type=F32)


def _in_proj_kernel(x_ref, g_ref, w_ref, o_ref, hn_ref):
    @pl.when(pl.program_id(1) == 0)
    def _():
        hn_ref[...] = _rmsnorm(x_ref[...], g_ref[...]).astype(BF16)

    o_ref[...] = _dot(hn_ref[...], w_ref[...]).astype(BF16)


def _in_projection(x, g, w, *, tm=1024, tn=2048):
    S, D = x.shape
    N = w.shape[1]
    return pl.pallas_call(
        _in_proj_kernel,
        grid=(S // tm, N // tn),
        in_specs=[pl.BlockSpec((tm, D), lambda i, j: (i, 0)),
                  pl.BlockSpec((1, D), lambda i, j: (0, 0)),
                  pl.BlockSpec((D, tn), lambda i, j: (0, j))],
        out_specs=pl.BlockSpec((tm, tn), lambda i, j: (i, j)),
        out_shape=jax.ShapeDtypeStruct((S, N), BF16),
        scratch_shapes=[pltpu.VMEM((tm, D), BF16)],
        compiler_params=_params("parallel", "arbitrary"),
        name="in_projection",
    )(x, g, w)


def _retention_consts():
    C, H = RET_CHUNK, RET_HEADS
    log_g = np.log1p(-(2.0 ** (-5.0 - np.arange(H, dtype=np.float64))))
    idx = np.arange(C, dtype=np.float64)
    dist = idx[:, None] - idx[None, :]
    intra = np.where(dist[None] >= 0, np.exp(np.maximum(dist, 0.0)[None] * log_g[:, None, None]), 0.0)
    kdec = np.exp((C - 1 - idx)[None, :, None] * log_g[:, None, None]) * np.ones((1, 1, RET_D))
    qdec = np.exp((idx + 1.0)[None, :, None] * log_g[:, None, None]) * np.ones((1, 1, RET_D))
    chunk_decay = tuple(float(v) for v in np.exp(C * log_g))
    return (jnp.asarray(intra, F32), jnp.asarray(qdec, F32), jnp.asarray(kdec, F32), chunk_decay)


def _rope(x, cos2, sin2):
    return x * cos2 + pltpu.roll(x, RET_D // 2, 1) * sin2


def _retention_kernel(q_ref, k_ref, v_ref, g_ref, cos_ref, sin_ref, intra_ref, qdec_ref, kdec_ref,
                      o_ref, state_ref, *, chunks, chunk_decay):
    @pl.when(pl.program_id(0) == 0)
    def _():
        state_ref[...] = jnp.zeros_like(state_ref)

    C = RET_CHUNK
    for c in range(chunks):
        rows = slice(c * C, (c + 1) * C)
        cos2 = cos_ref[rows, :]
        sin2 = sin_ref[rows, :]
        for h in range(RET_HEADS):
            cols = slice(h * RET_D, (h + 1) * RET_D)
            q = _rope(q_ref[rows, cols].astype(F32), cos2, sin2)
            k = _rope(k_ref[rows, cols].astype(F32), cos2, sin2)
            v = v_ref[rows, cols]
            state = state_ref[h]
            scores = _dot_nt(q.astype(BF16), k.astype(BF16)) * intra_ref[h]
            o = _dot(scores.astype(BF16), v) + _dot((q * qdec_ref[h]).astype(BF16), state.astype(BF16))
            kv = _dot((k * kdec_ref[h]).T.astype(BF16), v)
            state_ref[h] = state * chunk_decay[h] + kv
            gate = g_ref[rows, cols].astype(F32)
            o_ref[rows, cols] = (gate * jax.nn.sigmoid(gate) * _rmsnorm(o)).astype(BF16)


def _retention(proj, cos2, sin2, *, chunks=2):
    S = proj.shape[0]
    W = RET_HEADS * RET_D
    R = chunks * RET_CHUNK
    intra, qdec, kdec, chunk_decay = _retention_consts()
    col = lambda b: pl.BlockSpec((R, W), lambda i, b=b: (i, b))
    const = pl.BlockSpec((RET_HEADS, RET_CHUNK, RET_D), lambda i: (0, 0, 0))
    rope_spec = pl.BlockSpec((R, RET_D), lambda i: (i, 0))
    return pl.pallas_call(
        functools.partial(_retention_kernel, chunks=chunks, chunk_decay=chunk_decay),
        grid=(S // R,),
        in_specs=[col(0), col(1), col(2), col(3), rope_spec, rope_spec, const, const, const],
        out_specs=pl.BlockSpec((R, W), lambda i: (i, 0)),
        out_shape=jax.ShapeDtypeStruct((S, W), BF16),
        scratch_shapes=[pltpu.VMEM((RET_HEADS, RET_D, RET_D), F32)],
        compiler_params=_params("arbitrary"),
        name="retention",
    )(proj, proj, proj, proj, cos2, sin2, intra, qdec, kdec)


def _t5_bucket(dist):
    max_exact = T5_BUCKETS // 2
    d = jnp.maximum(dist, 0)
    df = jnp.maximum(d, 1).astype(F32)
    large = max_exact + (jnp.log(df / max_exact) / math.log(T5_MAX_DIST / max_exact)
                         * (T5_BUCKETS - max_exact)).astype(jnp.int32)
    large = jnp.minimum(large, T5_BUCKETS - 1)
    return jnp.where(d < max_exact, d, large)


def _near_bias_tiles(t5_bias, t):
    assert t >= T5_MAX_DIST
    ql = jnp.arange(t, dtype=jnp.int32)[:, None]
    kl = jnp.arange(t, dtype=jnp.int32)[None, :]
    tab = t5_bias.T.astype(F32) * LOG2E
    tiles = []
    for off in (0, 1):
        dist = ql + off * t - kl
        bucket = _t5_bucket(dist)[None]
        b = jnp.zeros((tab.shape[0], t, t), F32)
        for n in range(T5_BUCKETS):
            b = jnp.where(bucket == n, tab[:, n, None, None], b)
        tiles.append(jnp.where(dist[None] >= 0, b, NEG_BIG))
    return jnp.stack(tiles, axis=1)


def _diff_attn_kernel(t5_ref, lam_ref, q_ref, k_ref, v_ref, bias_ref, subg_ref, o_ref,
                      mx_ref, l_ref, acc_ref, *, t, lam_init):
    h = pl.program_id(0)
    i = pl.program_id(1)
    far_bias = t5_ref[h, T5_BUCKETS - 1] * LOG2E
    q = q_ref[...]

    def key_block(j):
        return pl.ds(pl.multiple_of(j * t, t), t)

    def online_block(j, bias, shift):
        k = k_ref[key_block(j), :]
        v = v_ref[key_block(j), :]
        for mp in range(2):
            cols = slice(mp * DIFF_D, (mp + 1) * DIFF_D)
            s = _dot_nt(q[:, cols], k[:, cols])
            if bias is not None:
                s = s + bias
            groups = [s[:, g * LANES:(g + 1) * LANES] for g in range(t // LANES)]
            bm = functools.reduce(jnp.maximum, groups)
            m_old = mx_ref[mp]
            m_new = jnp.maximum(m_old, jnp.max(bm, axis=-1, keepdims=True) + shift)
            alpha = jnp.exp2(m_old - m_new)
            ms = m_new - shift
            ps = [jnp.exp2(g - ms) for g in groups]
            l_ref[mp] = alpha * l_ref[mp] + functools.reduce(lambda a, b: a + b, ps)
            acc_ref[mp] = (jnp.concatenate([alpha, alpha], axis=-1) * acc_ref[mp]
                           + _dot(jnp.concatenate(ps, axis=-1).astype(BF16), v))
            mx_ref[mp] = m_new

    mx_ref[...] = jnp.full_like(mx_ref, NEG_BIG)
    l_ref[...] = jnp.zeros_like(l_ref)
    acc_ref[...] = jnp.zeros_like(acc_ref)

    n = jnp.maximum(i - 1, 0)

    def far_body(jj, carry):
        for u in range(FAR_UNROLL):
            online_block(jj * FAR_UNROLL + u, None, far_bias)
        return carry

    lax.fori_loop(0, n // FAR_UNROLL, far_body, 0)
    for r in range(1, FAR_UNROLL):
        @pl.when(n % FAR_UNROLL >= r)
        def _():
            online_block(n - (n % FAR_UNROLL) + r - 1, None, far_bias)

    @pl.when(i >= 1)
    def _():
        online_block(i - 1, bias_ref[1], 0.0)

    online_block(i, bias_ref[0], 0.0)

    lam_p = lam_ref[...]
    lam = (jnp.exp(jnp.sum(lam_p[0:1] * lam_p[1:2], axis=-1, keepdims=True))
           - jnp.exp(jnp.sum(lam_p[2:3] * lam_p[3:4], axis=-1, keepdims=True)) + lam_init)
    l0 = jnp.sum(l_ref[0], axis=-1, keepdims=True)
    l1 = jnp.sum(l_ref[1], axis=-1, keepdims=True)
    o = acc_ref[0] / l0 - lam * (acc_ref[1] / l1)
    o_ref[...] = (_rmsnorm(o, subg_ref[...]) * (1.0 - lam_init)).astype(BF16)


def _diff_attention(proj, bias, t5_bias, lam_params, subln_g, lam_init, *, col0, t):
    S = proj.shape[0]
    HW = 2 * DIFF_D
    qb = col0 // HW
    kb = qb + DIFF_HEADS
    vb = kb + DIFF_HEADS
    return pl.pallas_call(
        functools.partial(_diff_attn_kernel, t=t, lam_init=lam_init),
        grid=(DIFF_HEADS, S // t),
        in_specs=[pl.BlockSpec(memory_space=pltpu.SMEM),
                  pl.BlockSpec((4, DIFF_D), lambda h, i: (0, 0)),
                  pl.BlockSpec((t, HW), lambda h, i: (i, qb + h)),
                  pl.BlockSpec((S, HW), lambda h, i: (0, kb + h)),
                  pl.BlockSpec((S, HW), lambda h, i: (0, vb + h)),
                  pl.BlockSpec((None, 2, t, t), lambda h, i: (h, 0, 0, 0)),
                  pl.BlockSpec((1, HW), lambda h, i: (0, 0))],
        out_specs=pl.BlockSpec((t, HW), lambda h, i: (i, h)),
        out_shape=jax.ShapeDtypeStruct((S, DIFF_HEADS * HW), BF16),
        scratch_shapes=[pltpu.VMEM((2, t, LANES), F32), pltpu.VMEM((2, t, LANES), F32),
                        pltpu.VMEM((2, t, HW), F32)],
        compiler_params=_params("arbitrary", "arbitrary"),
        name="diff_attention",
    )(t5_bias.T.astype(F32), lam_params, proj, proj, proj, bias, subln_g)


def _merge_kernel(ya_ref, cb_ref, cc_ref, cu_ref, hc_ref, hu_ref, wconv_ref, yc_ref,
                  g0_ref, g1_ref, g2_ref, wb_ref, o_ref, *, halo):
    i = pl.program_id(0)
    v = cc_ref[...].astype(F32) * cu_ref[...].astype(F32)
    hv = jnp.where(i > 0, hc_ref[...].astype(F32) * hu_ref[...].astype(F32), 0.0)
    prev1 = hv[halo - 1:halo, :]
    prev2 = hv[halo - 2:halo - 1, :]
    row = lax.broadcasted_iota(jnp.int32, v.shape, 0)
    v1 = jnp.where(row == 0, prev1, pltpu.roll(v, 1, 0))
    v2 = jnp.where(row == 0, prev2, jnp.where(row == 1, prev1, pltpu.roll(v, 2, 0)))
    w = wconv_ref[...]
    yb = cb_ref[...].astype(F32) * (w[0:1] * v2 + w[1:2] * v1 + w[2:3] * v)
    merged = jax.nn.sigmoid(g0_ref[...].astype(F32)) * _dot(ya_ref[...], wb_ref[0])
    merged += jax.nn.sigmoid(g1_ref[...].astype(F32)) * _dot(yb.astype(BF16), wb_ref[1])
    merged += jax.nn.sigmoid(g2_ref[...].astype(F32)) * _dot(yc_ref[...], wb_ref[2])
    o_ref[...] = merged.astype(BF16)


def _merge(ya, yc, proj, w_conv, w_branch, *, conv_col0, gate_col0, tm=256, halo=16):
    S = proj.shape[0]
    W = BRANCH_W
    D = w_branch.shape[-1]
    cblk = conv_col0 // W
    gblk = gate_col0 // D
    hb = tm // halo
    row = lambda b: pl.BlockSpec((tm, W), lambda i, b=b: (i, b))
    halo_spec = lambda b: pl.BlockSpec((halo, W), lambda i, b=b: (jnp.maximum(i * hb - 1, 0), b))
    gate = lambda b: pl.BlockSpec((tm, D), lambda i, b=b: (i, b))
    return pl.pallas_call(
        functools.partial(_merge_kernel, halo=halo),
        grid=(S // tm,),
        in_specs=[pl.BlockSpec((tm, W), lambda i: (i, 0)),
                  row(cblk), row(cblk + 1), row(cblk + 2), halo_spec(cblk + 1), halo_spec(cblk + 2),
                  pl.BlockSpec((CONV_K, W), lambda i: (0, 0)),
                  pl.BlockSpec((tm, W), lambda i: (i, 0)),
                  gate(gblk), gate(gblk + 1), gate(gblk + 2),
                  pl.BlockSpec((N_BRANCH, W, D), lambda i: (0, 0, 0))],
        out_specs=pl.BlockSpec((tm, D), lambda i: (i, 0)),
        out_shape=jax.ShapeDtypeStruct((S, D), BF16),
        compiler_params=_params("parallel"),
        name="conv_gated_merge",
    )(ya, proj, proj, proj, proj, proj, w_conv, yc, proj, proj, proj, w_branch)


def _proj_residual_kernel(a_ref, w_ref, g_ref, x_ref, o_ref):
    o_ref[...] = x_ref[...] + _rmsnorm(_dot(a_ref[...], w_ref[...]), g_ref[...])


def _proj_residual(a, w, g, x, *, tm=512):
    S, K = a.shape
    D = w.shape[1]
    return pl.pallas_call(
        _proj_residual_kernel,
        grid=(S // tm,),
        in_specs=[pl.BlockSpec((tm, K), lambda i: (i, 0)),
                  pl.BlockSpec((K, D), lambda i: (0, 0)),
                  pl.BlockSpec((1, D), lambda i: (0, 0)),
                  pl.BlockSpec((tm, D), lambda i: (i, 0))],
        out_specs=pl.BlockSpec((tm, D), lambda i: (i, 0)),
        out_shape=jax.ShapeDtypeStruct((S, D), F32),
        compiler_params=_params("parallel"),
        name="proj_norm_residual",
    )(a, w, g, x)


def _mem_kv_kernel(mem_ref, g_ref, w_ref, o_ref):
    o_ref[...] = _dot(_rmsnorm(mem_ref[...], g_ref[...]).astype(BF16), w_ref[...]).astype(BF16)


def _mem_kv(mem, g, wkv):
    M, D = mem.shape
    N = wkv.shape[1]
    return pl.pallas_call(
        _mem_kv_kernel,
        out_shape=jax.ShapeDtypeStruct((M, N), BF16),
        compiler_params=pltpu.CompilerParams(vmem_limit_bytes=VMEM_LIMIT),
        name="memory_kv",
    )(mem, g, wkv)


def _cross_attn_kernel(x_ref, gpre_ref, wq_ref, kv_ref, wo_ref, gpost_ref, o_ref):
    x = x_ref[...]
    q = _dot(_rmsnorm(x, gpre_ref[...]).astype(BF16), wq_ref[...]).astype(BF16)
    width = X_HEADS * X_HEAD_DIM
    outs = []
    for h in range(X_HEADS):
        cols = slice(h * X_HEAD_DIM, (h + 1) * X_HEAD_DIM)
        k = kv_ref[:, cols]
        v = kv_ref[:, width + h * X_HEAD_DIM: width + (h + 1) * X_HEAD_DIM]
        s = _dot_nt(q[:, cols], k)
        p = jnp.exp(s - jnp.max(s, axis=-1, keepdims=True))
        p = p / jnp.sum(p, axis=-1, keepdims=True)
        outs.append(_dot(p.astype(BF16), v).astype(BF16))
    o = jnp.concatenate(outs, axis=-1)
    o_ref[...] = x + _rmsnorm(_dot(o, wo_ref[...]), gpost_ref[...])


def _cross_attention(x, g_pre, wq, kv, wo, g_post, *, tm=512):
    S, D = x.shape
    full = lambda a: pl.BlockSpec(a.shape, lambda i: (0,) * a.ndim)
    return pl.pallas_call(
        _cross_attn_kernel,
        grid=(S // tm,),
        in_specs=[pl.BlockSpec((tm, D), lambda i: (i, 0)), full(g_pre), full(wq), full(kv), full(wo), full(g_post)],
        out_specs=pl.BlockSpec((tm, D), lambda i: (i, 0)),
        out_shape=jax.ShapeDtypeStruct((S, D), F32),
        compiler_params=_params("parallel"),
        name="cross_attention",
    )(x, g_pre, wq, kv, wo, g_post)


def _swiglu_partial(hn, wg, wu, wd):
    a = _dot(hn, wg)
    u = _dot(hn, wu)
    return _dot((a * jax.nn.sigmoid(a) * u).astype(BF16), wd)


def _ffn_kernel(x_ref, gpre_ref, wg_ref, wu_ref, wd_ref, gpost_ref, o_ref, hn_ref, acc_ref):
    f = pl.program_id(1)

    @pl.when(f == 0)
    def _():
        hn_ref[...] = _rmsnorm(x_ref[...], gpre_ref[...]).astype(BF16)
        acc_ref[...] = jnp.zeros_like(acc_ref)

    acc_ref[...] += _swiglu_partial(hn_ref[...], wg_ref[...], wu_ref[...], wd_ref[...])

    @pl.when(f == pl.num_programs(1) - 1)
    def _():
        o_ref[...] = x_ref[...] + _rmsnorm(acc_ref[...], gpost_ref[...])


def _ffn(x, g_pre, wg, wu, wd, g_post, *, tm=512, tf=FFN_TF):
    S, D = x.shape
    F = wg.shape[1]
    return pl.pallas_call(
        _ffn_kernel,
        grid=(S // tm, F // tf),
        in_specs=[pl.BlockSpec((tm, D), lambda i, f: (i, 0)),
                  pl.BlockSpec((1, D), lambda i, f: (0, 0)),
                  pl.BlockSpec((D, tf), lambda i, f: (0, f)),
                  pl.BlockSpec((D, tf), lambda i, f: (0, f)),
                  pl.BlockSpec((tf, D), lambda i, f: (f, 0)),
                  pl.BlockSpec((1, D), lambda i, f: (0, 0))],
        out_specs=pl.BlockSpec((tm, D), lambda i, f: (i, 0)),
        out_shape=jax.ShapeDtypeStruct((S, D), F32),
        scratch_shapes=[pltpu.VMEM((tm, D), BF16), pltpu.VMEM((tm, D), F32)],
        compiler_params=_params("parallel", "arbitrary"),
        name="swiglu_ffn",
    )(x, g_pre, wg, wu, wd, g_post)


def _router_kernel(x_ref, g_ref, wr_ref, sel_ref, gate_ref):
    hn = _rmsnorm(x_ref[...], g_ref[...])
    logits = jnp.dot(hn, wr_ref[...], preferred_element_type=F32, precision=lax.Precision.HIGHEST)
    lane = lax.broadcasted_iota(jnp.int32, logits.shape, 1)
    neg_inf = -jnp.inf
    lg = jnp.where(lane < N_EXPERTS, logits, neg_inf)
    m1 = jnp.max(lg, axis=-1, keepdims=True)
    i1 = jnp.min(jnp.where(lg == m1, lane, LANES), axis=-1, keepdims=True)
    lg2 = jnp.where(lane == i1, neg_inf, lg)
    m2 = jnp.max(lg2, axis=-1, keepdims=True)
    i2 = jnp.min(jnp.where(lg2 == m2, lane, LANES), axis=-1, keepdims=True)
    e = jnp.exp(m2 - m1)
    g1 = 1.0 / (1.0 + e)
    sel_ref[...] = jnp.where(lane == 0, i1, jnp.where(lane == 1, i2, 0))
    gate_ref[...] = jnp.where(lane == 0, g1, jnp.where(lane == 1, e * g1, 0.0))


def _router(x, g_pre, w_router_padded, *, tm=512):
    S, D = x.shape
    return pl.pallas_call(
        _router_kernel,
        grid=(S // tm,),
        in_specs=[pl.BlockSpec((tm, D), lambda i: (i, 0)),
                  pl.BlockSpec((1, D), lambda i: (0, 0)),
                  pl.BlockSpec((D, LANES), lambda i: (0, 0))],
        out_specs=[pl.BlockSpec((tm, LANES), lambda i: (i, 0)),
                   pl.BlockSpec((tm, LANES), lambda i: (i, 0))],
        out_shape=[jax.ShapeDtypeStruct((S, LANES), jnp.int32),
                   jax.ShapeDtypeStruct((S, LANES), F32)],
        compiler_params=_params("parallel"),
        name="moe_router",
    )(x, g_pre, w_router_padded)


def _moe_plan(sel, gate, tg):
    S = sel.shape[0]
    P, E = 2 * S, N_EXPERTS
    i32 = jnp.int32
    e_flat = sel[:, :2].reshape(P)
    g_flat = gate[:, :2].reshape(P)
    onehot = (e_flat[:, None] == jnp.arange(E, dtype=i32)[None, :]).astype(i32)
    counts = jnp.sum(onehot, axis=0)
    rank = jnp.sum((jnp.cumsum(onehot, axis=0) - 1) * onehot, axis=1)
    ustart = jnp.cumsum(counts) - counts
    tiles = (counts + tg - 1) // tg
    tile_end = jnp.cumsum(tiles)
    tstart = tile_end - tiles
    n_tiles = P // tg + E
    n = jnp.arange(n_tiles, dtype=i32)
    tile_e = jnp.minimum(jnp.sum((n[:, None] >= tile_end[None, :]).astype(i32), axis=1), E - 1)
    used = n < tile_end[-1]
    local = ((n - tstart[tile_e]) * tg)[:, None] + jnp.arange(tg, dtype=i32)[None, :]
    valid = used[:, None] & (local < counts[tile_e][:, None])
    order = jnp.argsort(e_flat, stable=True).astype(i32)
    pair = order[jnp.clip(ustart[tile_e][:, None] + local, 0, P - 1)]
    src_token = jnp.where(valid, pair // 2, 0).reshape(-1)
    row_gate = jnp.where(valid, g_flat[pair], 0.0).reshape(-1, 1)
    dest = tstart[e_flat] * tg + rank
    return tile_e, used.astype(i32), src_token, row_gate, dest


def _row_gather(idx_ref, base, src_hbm, dst_ref, sem, n_rows, *, wait):
    def body(r, carry):
        cp = pltpu.make_async_copy(src_hbm.at[pl.ds(idx_ref[base + r], 1), :], dst_ref.at[pl.ds(r, 1), :], sem)
        if wait:
            cp.wait()
        else:
            cp.start()
        return carry

    lax.fori_loop(0, n_rows, body, 0, unroll=8)


def _moe_group_kernel(te_ref, used_ref, src_ref, x_hbm, gpre_ref, gate_ref, wg_ref, wu_ref, wd_ref, y_ref,
                      xg_ref, hb_ref, acc_ref, sem, *, tg):
    n = pl.program_id(0)
    f = pl.program_id(1)
    slot = n % 2

    @pl.when(f == 0)
    def _():
        @pl.when(n == 0)
        def _():
            _row_gather(src_ref, 0, x_hbm, xg_ref.at[0], sem.at[0], tg, wait=False)

        @pl.when(n + 1 < pl.num_programs(0))
        def _():
            _row_gather(src_ref, (n + 1) * tg, x_hbm, xg_ref.at[1 - slot], sem.at[1 - slot], tg, wait=False)

        _row_gather(src_ref, n * tg, x_hbm, xg_ref.at[slot], sem.at[slot], tg, wait=True)
        hb_ref[...] = _rmsnorm(xg_ref[slot], gpre_ref[...]).astype(BF16)
        acc_ref[...] = jnp.zeros_like(acc_ref)

    @pl.when(used_ref[n] > 0)
    def _():
        acc_ref[...] += _swiglu_partial(hb_ref[...], wg_ref[...], wu_ref[...], wd_ref[...])

    @pl.when(f == pl.num_programs(1) - 1)
    def _():
        y_ref[...] = acc_ref[...] * gate_ref[...]


def _moe_group(x, g_pre, tile_e, used, src_token, row_gate, wg, wu, wd, *, tg, tf=FFN_TF):
    S, D = x.shape
    E, _, F = wg.shape
    n_tiles = tile_e.shape[0]
    nf = F // tf
    fblk = lambda n, f, used: jnp.where(used[n] > 0, f, nf - 1)
    grid_spec = pltpu.PrefetchScalarGridSpec(
        num_scalar_prefetch=3,
        grid=(n_tiles, nf),
        in_specs=[pl.BlockSpec(memory_space=pl.ANY),
                  pl.BlockSpec((1, D), lambda n, f, te, used, src: (0, 0)),
                  pl.BlockSpec((tg, 1), lambda n, f, te, used, src: (n, 0)),
                  pl.BlockSpec((None, D, tf), lambda n, f, te, used, src: (te[n], 0, fblk(n, f, used))),
                  pl.BlockSpec((None, D, tf), lambda n, f, te, used, src: (te[n], 0, fblk(n, f, used))),
                  pl.BlockSpec((None, tf, D), lambda n, f, te, used, src: (te[n], fblk(n, f, used), 0))],
        out_specs=pl.BlockSpec((tg, D), lambda n, f, te, used, src: (n, 0)),
        scratch_shapes=[pltpu.VMEM((2, tg, D), F32), pltpu.VMEM((tg, D), BF16), pltpu.VMEM((tg, D), F32),
                        pltpu.SemaphoreType.DMA((2,))],
    )
    return pl.pallas_call(
        functools.partial(_moe_group_kernel, tg=tg),
        grid_spec=grid_spec,
        out_shape=jax.ShapeDtypeStruct((n_tiles * tg, D), F32),
        compiler_params=_params("arbitrary", "arbitrary"),
        name="moe_grouped_swiglu",
    )(tile_e, used, src_token, x, g_pre, row_gate, wg, wu, wd)


def _moe_combine_kernel(dest_ref, x_ref, y_hbm, g_ref, o_ref, yb_ref, sem, *, tm):
    i = pl.program_id(0)
    slot = i % 2

    def gather(tile, s, wait):
        def body(r, carry):
            for k in range(2):
                row = dest_ref[2 * (tile * tm + r) + k]
                cp = pltpu.make_async_copy(y_hbm.at[pl.ds(row, 1), :], yb_ref.at[s, k, pl.ds(r, 1), :], sem.at[s])
                if wait:
                    cp.wait()
                else:
                    cp.start()
            return carry

        lax.fori_loop(0, tm, body, 0, unroll=4)

    @pl.when(i == 0)
    def _():
        gather(0, 0, False)

    @pl.when(i + 1 < pl.num_programs(0))
    def _():
        gather(i + 1, 1 - slot, False)

    gather(i, slot, True)
    o_ref[...] = x_ref[...] + _rmsnorm(yb_ref[slot, 0] + yb_ref[slot, 1], g_ref[...])


def _moe_combine(x, y, dest, g_post, *, tm=256):
    S, D = x.shape
    grid_spec = pltpu.PrefetchScalarGridSpec(
        num_scalar_prefetch=1,
        grid=(S // tm,),
        in_specs=[pl.BlockSpec((tm, D), lambda i, dest: (i, 0)),
                  pl.BlockSpec(memory_space=pl.ANY),
                  pl.BlockSpec((1, D), lambda i, dest: (0, 0))],
        out_specs=pl.BlockSpec((tm, D), lambda i, dest: (i, 0)),
        scratch_shapes=[pltpu.VMEM((2, 2, tm, D), F32), pltpu.SemaphoreType.DMA((2,))],
    )
    return pl.pallas_call(
        functools.partial(_moe_combine_kernel, tm=tm),
        grid_spec=grid_spec,
        out_shape=jax.ShapeDtypeStruct((S, D), F32),
        compiler_params=_params("arbitrary"),
        name="moe_combine",
    )(dest, x, y, g_post)


def _rope_tables(S):
    half = RET_D // 2
    inv = ROPE_BASE ** (-jnp.arange(half, dtype=F32) / half)
    ang = jnp.arange(S, dtype=jnp.int32)[:, None].astype(F32) * inv[None, :]
    cos, sin = jnp.cos(ang), jnp.sin(ang)
    return jnp.concatenate([cos, cos], axis=-1), jnp.concatenate([-sin, sin], axis=-1)


def _scaled_in_weights(w):
    ret_w = RET_HEADS * RET_D
    diff_col0 = 4 * ret_w + 3 * CONV_WIDTH
    diff_w = DIFF_HEADS * 2 * DIFF_D
    scale = jnp.ones((w.shape[1],), F32)
    scale = scale.at[:ret_w].set(RET_D ** -0.5)
    scale = scale.at[diff_col0:diff_col0 + diff_w].set(DIFF_D ** -0.5 * LOG2E)
    return (w * scale[None, :]).astype(BF16)


def kernel(x, mem, t5_bias, w_in, w_conv, diff_lambda, diff_subln, w_branch, w_mix_out, w_xq, w_xkv, w_xo, w_ffn_gate, w_ffn_up, w_ffn_down, w_router, w_exp_gate, w_exp_up, w_exp_down, g_pre_mix, g_post_mix, g_pre_xattn, g_mem, g_post_xattn, g_pre_ffn, g_post_ffn):
    B, S, D = x.shape
    assert B == 1
    depth = w_in.shape[0]
    ret_w = RET_HEADS * RET_D
    conv_col0 = 4 * ret_w
    diff_col0 = conv_col0 + 3 * CONV_WIDTH
    gate_col0 = diff_col0 + 3 * DIFF_HEADS * 2 * DIFF_D
    cos2, sin2 = _rope_tables(S)
    near_bias = _near_bias_tiles(t5_bias, ATTN_BLOCK)
    xs = x[0]
    mem2 = mem[0]
    row = lambda g: g[None, :]
    for l in range(depth):
        proj = _in_projection(xs, row(g_pre_mix[l]), _scaled_in_weights(w_in[l]))
        ya = _retention(proj, cos2, sin2)
        lam_init = 0.8 - 0.6 * math.exp(-0.3 * l)
        yc = _diff_attention(proj, near_bias, t5_bias, diff_lambda[l], row(diff_subln[l]), lam_init,
                             col0=diff_col0, t=ATTN_BLOCK)
        merged = _merge(ya, yc, proj, w_conv[l], w_branch[l].astype(BF16),
                        conv_col0=conv_col0, gate_col0=gate_col0)
        xs = _proj_residual(merged, w_mix_out[l].astype(BF16), row(g_post_mix[l]), xs)
        kv = _mem_kv(mem2, row(g_mem[l]), w_xkv[l].astype(BF16))
        xs = _cross_attention(xs, row(g_pre_xattn[l]), (w_xq[l] * X_HEAD_DIM ** -0.5).astype(BF16), kv,
                              w_xo[l].astype(BF16), row(g_post_xattn[l]))
        if l % 2 == 0:
            xs = _ffn(xs, row(g_pre_ffn[l]), w_ffn_gate[l // 2].astype(BF16), w_ffn_up[l // 2].astype(BF16),
                      w_ffn_down[l // 2].astype(BF16), row(g_post_ffn[l]))
        else:
            wr = jnp.pad(w_router[l // 2], ((0, 0), (0, LANES - N_EXPERTS)))
            sel, gate = _router(xs, row(g_pre_ffn[l]), wr)
            tile_e, used, src_token, row_gate, dest = _moe_plan(sel, gate, MOE_TILE)
            y = _moe_group(xs, row(g_pre_ffn[l]), tile_e, used, src_token, row_gate, w_exp_gate[l // 2].astype(BF16),
                           w_exp_up[l // 2].astype(BF16), w_exp_down[l // 2].astype(BF16), tg=MOE_TILE)
            xs = _moe_combine(xs, y, dest, row(g_post_ffn[l]))
    return xs[None]
```

```python
import functools
import math

import numpy as np
import jax
import jax.numpy as jnp
from jax import lax
from jax.experimental import pallas as pl
from jax.experimental.pallas import tpu as pltpu

F32 = jnp.float32
BF16 = jnp.bfloat16

EPS = 1e-6
RET_HEADS = 8
RET_D = 128
RET_CHUNK = 128
ROPE_BASE = 10000.0
CONV_WIDTH = 1024
CONV_K = 3
DIFF_HEADS = 4
DIFF_D = 128
T5_BUCKETS = 32
T5_MAX_DIST = 128
X_HEADS = 4
X_HEAD_DIM = 128
N_EXPERTS = 8
N_BRANCH = 3
BRANCH_W = 1024

V7X_VMEM_BYTES = 64 * 1024 * 1024
VMEM_LIMIT = V7X_VMEM_BYTES - 8 * 1024 * 1024
LANES = 128
NEG_BIG = -1e30
LOG2E = math.log2(math.e)
ATTN_BLOCK = 512
FAR_UNROLL = 4
MOE_TILE = 512
FFN_TF = 1024


def _params(*semantics):
    return pltpu.CompilerParams(dimension_semantics=semantics, vmem_limit_bytes=VMEM_LIMIT)


def _rmsnorm(x, g=None):
    y = x * lax.rsqrt(jnp.mean(x * x, axis=-1, keepdims=True) + EPS)
    return y if g is None else y * g


def _dot(a, b):
    return jnp.dot(a, b, preferred_element_type=F32)


def _dot_nt(a, b):
    return lax.dot_general(a, b, (((1,), (1,)), ((), ())), preferred_element_type=F32)


def _in_proj_kernel(x_ref, g_ref, w_ref, o_ref, hn_ref):
    @pl.when(pl.program_id(1) == 0)
    def _():
        hn_ref[...] = _rmsnorm(x_ref[...], g_ref[...]).astype(BF16)

    o_ref[...] = _dot(hn_ref[...], w_ref[...]).astype(BF16)


def _in_projection(x, g, w, *, tm=1024, tn=2048):
    S, D = x.shape
    N = w.shape[1]
    return pl.pallas_call(
        _in_proj_kernel,
        grid=(S // tm, N // tn),
        in_specs=[pl.BlockSpec((tm, D), lambda i, j: (i, 0)),
                  pl.BlockSpec((1, D), lambda i, j: (0, 0)),
                  pl.BlockSpec((D, tn), lambda i, j: (0, j))],
        out_specs=pl.BlockSpec((tm, tn), lambda i, j: (i, j)),
        out_shape=jax.ShapeDtypeStruct((S, N), BF16),
        scratch_shapes=[pltpu.VMEM((tm, D), BF16)],
        compiler_params=_params("parallel", "arbitrary"),
        name="in_projection",
    )(x, g, w)


def _retention_consts():
    C, H = RET_CHUNK, RET_HEADS
    log_g = np.log1p(-(2.0 ** (-5.0 - np.arange(H, dtype=np.float64))))
    idx = np.arange(C, dtype=np.float64)
    dist = idx[:, None] - idx[None, :]
    intra = np.where(dist[None] >= 0, np.exp(np.maximum(dist, 0.0)[None] * log_g[:, None, None]), 0.0)
    kdec = np.exp((C - 1 - idx)[None, :, None] * log_g[:, None, None]) * np.ones((1, 1, RET_D))
    qdec = np.exp((idx + 1.0)[None, :, None] * log_g[:, None, None]) * np.ones((1, 1, RET_D))
    chunk_decay = tuple(float(v) for v in np.exp(C * log_g))
    return (jnp.asarray(intra, F32), jnp.asarray(qdec, F32), jnp.asarray(kdec, F32), chunk_decay)


def _rope(x, cos2, sin2):
    return x * cos2 + pltpu.roll(x, RET_D // 2, 1) * sin2


def _retention_kernel(q_ref, k_ref, v_ref, g_ref, cos_ref, sin_ref, intra_ref, qdec_ref, kdec_ref,
                      o_ref, state_ref, *, chunks, chunk_decay):
    @pl.when(pl.program_id(0) == 0)
    def _():
        state_ref[...] = jnp.zeros_like(state_ref)

    C = RET_CHUNK
    for c in range(chunks):
        rows = slice(c * C, (c + 1) * C)
        cos2 = cos_ref[rows, :]
        sin2 = sin_ref[rows, :]
        for h in range(RET_HEADS):
            cols = slice(h * RET_D, (h + 1) * RET_D)
            q = _rope(q_ref[rows, cols].astype(F32), cos2, sin2)
            k = _rope(k_ref[rows, cols].astype(F32), cos2, sin2)
            v = v_ref[rows, cols]
            state = state_ref[h]
            scores = _dot_nt(q.astype(BF16), k.astype(BF16)) * intra_ref[h]
            o = _dot(scores.astype(BF16), v) + _dot((q * qdec_ref[h]).astype(BF16), state.astype(BF16))
            kv = _dot((k * kdec_ref[h]).T.astype(BF16), v)
            state_ref[h] = state * chunk_decay[h] + kv
            gate = g_ref[rows, cols].astype(F32)
            o_ref[rows, cols] = (gate * jax.nn.sigmoid(gate) * _rmsnorm(o)).astype(BF16)


def _retention(proj, cos2, sin2, *, chunks=2):
    S = proj.shape[0]
    W = RET_HEADS * RET_D
    R = chunks * RET_CHUNK
    intra, qdec, kdec, chunk_decay = _retention_consts()
    col = lambda b: pl.BlockSpec((R, W), lambda i, b=b: (i, b))
    const = pl.BlockSpec((RET_HEADS, RET_CHUNK, RET_D), lambda i: (0, 0, 0))
    rope_spec = pl.BlockSpec((R, RET_D), lambda i: (i, 0))
    return pl.pallas_call(
        functools.partial(_retention_kernel, chunks=chunks, chunk_decay=chunk_decay),
        grid=(S // R,),
        in_specs=[col(0), col(1), col(2), col(3), rope_spec, rope_spec, const, const, const],
        out_specs=pl.BlockSpec((R, W), lambda i: (i, 0)),
        out_shape=jax.ShapeDtypeStruct((S, W), BF16),
        scratch_shapes=[pltpu.VMEM((RET_HEADS, RET_D, RET_D), F32)],
        compiler_params=_params("arbitrary"),
        name="retention",
    )(proj, proj, proj, proj, cos2, sin2, intra, qdec, kdec)


def _t5_bucket(dist):
    max_exact = T5_BUCKETS // 2
    d = jnp.maximum(dist, 0)
    df = jnp.maximum(d, 1).astype(F32)
    large = max_exact + (jnp.log(df / max_exact) / math.log(T5_MAX_DIST / max_exact)
                         * (T5_BUCKETS - max_exact)).astype(jnp.int32)
    large = jnp.minimum(large, T5_BUCKETS - 1)
    return jnp.where(d < max_exact, d, large)


def _near_bias_tiles(t5_bias, t):
    assert t >= T5_MAX_DIST
    ql = jnp.arange(t, dtype=jnp.int32)[:, None]
    kl = jnp.arange(t, dtype=jnp.int32)[None, :]
    tab = t5_bias.T.astype(F32) * LOG2E
    tiles = []
    for off in (0, 1):
        dist = ql + off * t - kl
        bucket = _t5_bucket(dist)[None]
        b = jnp.zeros((tab.shape[0], t, t), F32)
        for n in range(T5_BUCKETS):
            b = jnp.where(bucket == n, tab[:, n, None, None], b)
        tiles.append(jnp.where(dist[None] >= 0, b, NEG_BIG))
    return jnp.stack(tiles, axis=1)


def _diff_attn_kernel(t5_ref, lam_ref, q_ref, k_ref, v_ref, bias_ref, subg_ref, o_ref,
                      mx_ref, l_ref, acc_ref, *, t, lam_init):
    h = pl.program_id(0)
    i = pl.program_id(1)
    far_bias = t5_ref[h, T5_BUCKETS - 1] * LOG2E
    q = q_ref[...]

    def key_block(j):
        return pl.ds(pl.multiple_of(j * t, t), t)

    def online_block(j, bias, shift):
        k = k_ref[key_block(j), :]
        v = v_ref[key_block(j), :]
        for mp in range(2):
            cols = slice(mp * DIFF_D, (mp + 1) * DIFF_D)
            s = _dot_nt(q[:, cols], k[:, cols])
            if bias is not None:
                s = s + bias
            groups = [s[:, g * LANES:(g + 1) * LANES] for g in range(t // LANES)]
            bm = functools.reduce(jnp.maximum, groups)
            m_old = mx_ref[mp]
            m_new = jnp.maximum(m_old, jnp.max(bm, axis=-1, keepdims=True) + shift)
            alpha = jnp.exp2(m_old - m_new)
            ms = m_new - shift
            ps = [jnp.exp2(g - ms) for g in groups]
            l_ref[mp] = alpha * l_ref[mp] + functools.reduce(lambda a, b: a + b, ps)
            acc_ref[mp] = (jnp.concatenate([alpha, alpha], axis=-1) * acc_ref[mp]
                           + _dot(jnp.concatenate(ps, axis=-1).astype(BF16), v))
            mx_ref[mp] = m_new

    mx_ref[...] = jnp.full_like(mx_ref, NEG_BIG)
    l_ref[...] = jnp.zeros_like(l_ref)
    acc_ref[...] = jnp.zeros_like(acc_ref)

    n = jnp.maximum(i - 1, 0)

    def far_body(jj, carry):
        for u in range(FAR_UNROLL):
            online_block(jj * FAR_UNROLL + u, None, far_bias)
        return carry

    lax.fori_loop(0, n // FAR_UNROLL, far_body, 0)
    for r in range(1, FAR_UNROLL):
        @pl.when(n % FAR_UNROLL == r)
        def _():
            for u in range(r):
                online_block(n - r + u, None, far_bias)

    @pl.when(i >= 1)
    def _():
        online_block(i - 1, bias_ref[1], 0.0)
        online_block(i, bias_ref[0], 0.0)

    @pl.when(i == 0)
    def _():
        online_block(i, bias_ref[0], 0.0)

    lam_p = lam_ref[...]
    lam = (jnp.exp(jnp.sum(lam_p[0:1] * lam_p[1:2], axis=-1, keepdims=True))
           - jnp.exp(jnp.sum(lam_p[2:3] * lam_p[3:4], axis=-1, keepdims=True)) + lam_init)
    l0 = jnp.sum(l_ref[0], axis=-1, keepdims=True)
    l1 = jnp.sum(l_ref[1], axis=-1, keepdims=True)
    o = acc_ref[0] / l0 - lam * (acc_ref[1] / l1)
    o_ref[...] = (_rmsnorm(o, subg_ref[...]) * (1.0 - lam_init)).astype(BF16)


def _diff_attention(proj, bias, t5_bias, lam_params, subln_g, lam_init, *, col0, t):
    S = proj.shape[0]
    HW = 2 * DIFF_D
    qb = col0 // HW
    kb = qb + DIFF_HEADS
    vb = kb + DIFF_HEADS
    return pl.pallas_call(
        functools.partial(_diff_attn_kernel, t=t, lam_init=lam_init),
        grid=(DIFF_HEADS, S // t),
        in_specs=[pl.BlockSpec(memory_space=pltpu.SMEM),
                  pl.BlockSpec((4, DIFF_D), lambda h, i: (0, 0)),
                  pl.BlockSpec((t, HW), lambda h, i: (i, qb + h)),
                  pl.BlockSpec((S, HW), lambda h, i: (0, kb + h)),
                  pl.BlockSpec((S, HW), lambda h, i: (0, vb + h)),
                  pl.BlockSpec((None, 2, t, t), lambda h, i: (h, 0, 0, 0)),
                  pl.BlockSpec((1, HW), lambda h, i: (0, 0))],
        out_specs=pl.BlockSpec((t, HW), lambda h, i: (i, h)),
        out_shape=jax.ShapeDtypeStruct((S, DIFF_HEADS * HW), BF16),
        scratch_shapes=[pltpu.VMEM((2, t, LANES), F32), pltpu.VMEM((2, t, LANES), F32),
                        pltpu.VMEM((2, t, HW), F32)],
        compiler_params=_params("arbitrary", "arbitrary"),
        name="diff_attention",
    )(t5_bias.T.astype(F32), lam_params, proj, proj, proj, bias, subln_g)


def _merge_kernel(ya_ref, cb_ref, cc_ref, cu_ref, hc_ref, hu_ref, wconv_ref, yc_ref,
                  g0_ref, g1_ref, g2_ref, wb_ref, o_ref, *, halo):
    i = pl.program_id(0)
    v = cc_ref[...].astype(F32) * cu_ref[...].astype(F32)
    hv = jnp.where(i > 0, hc_ref[...].astype(F32) * hu_ref[...].astype(F32), 0.0)
    prev1 = hv[halo - 1:halo, :]
    prev2 = hv[halo - 2:halo - 1, :]
    row = lax.broadcasted_iota(jnp.int32, v.shape, 0)
    v1 = jnp.where(row == 0, prev1, pltpu.roll(v, 1, 0))
    v2 = jnp.where(row == 0, prev2, jnp.where(row == 1, prev1, pltpu.roll(v, 2, 0)))
    w = wconv_ref[...]
    yb = cb_ref[...].astype(F32) * (w[0:1] * v2 + w[1:2] * v1 + w[2:3] * v)
    merged = jax.nn.sigmoid(g0_ref[...].astype(F32)) * _dot(ya_ref[...], wb_ref[0])
    merged += jax.nn.sigmoid(g1_ref[...].astype(F32)) * _dot(yb.astype(BF16), wb_ref[1])
    merged += jax.nn.sigmoid(g2_ref[...].astype(F32)) * _dot(yc_ref[...], wb_ref[2])
    o_ref[...] = merged.astype(BF16)


def _merge(ya, yc, proj, w_conv, w_branch, *, conv_col0, gate_col0, tm=256, halo=16):
    S = proj.shape[0]
    W = BRANCH_W
    D = w_branch.shape[-1]
    cblk = conv_col0 // W
    gblk = gate_col0 // D
    hb = tm // halo
    row = lambda b: pl.BlockSpec((tm, W), lambda i, b=b: (i, b))
    halo_spec = lambda b: pl.BlockSpec((halo, W), lambda i, b=b: (jnp.maximum(i * hb - 1, 0), b))
    gate = lambda b: pl.BlockSpec((tm, D), lambda i, b=b: (i, b))
    return pl.pallas_call(
        functools.partial(_merge_kernel, halo=halo),
        grid=(S // tm,),
        in_specs=[pl.BlockSpec((tm, W), lambda i: (i, 0)),
                  row(cblk), row(cblk + 1), row(cblk + 2), halo_spec(cblk + 1), halo_spec(cblk + 2),
                  pl.BlockSpec((CONV_K, W), lambda i: (0, 0)),
                  pl.BlockSpec((tm, W), lambda i: (i, 0)),
                  gate(gblk), gate(gblk + 1), gate(gblk + 2),
                  pl.BlockSpec((N_BRANCH, W, D), lambda i: (0, 0, 0))],
        out_specs=pl.BlockSpec((tm, D), lambda i: (i, 0)),
        out_shape=jax.ShapeDtypeStruct((S, D), BF16),
        compiler_params=_params("parallel"),
        name="conv_gated_merge",
    )(ya, proj, proj, proj, proj, proj, w_conv, yc, proj, proj, proj, w_branch)


def _mem_kv_kernel(mem_ref, g_ref, w_ref, o_ref):
    o_ref[...] = _dot(_rmsnorm(mem_ref[...], g_ref[...]).astype(BF16), w_ref[...]).astype(BF16)


def _mem_kv(mem, g, wkv):
    M, D = mem.shape
    N = wkv.shape[1]
    return pl.pallas_call(
        _mem_kv_kernel,
        out_shape=jax.ShapeDtypeStruct((M, N), BF16),
        compiler_params=pltpu.CompilerParams(vmem_limit_bytes=VMEM_LIMIT),
        name="memory_kv",
    )(mem, g, wkv)


def _mix_xattn_kernel(a_ref, wmix_ref, gmix_ref, x_ref, gpre_ref, wq_ref, kv_ref, wo_ref, gpost_ref, o_ref):
    x = x_ref[...] + _rmsnorm(_dot(a_ref[...], wmix_ref[...]), gmix_ref[...])
    q = _dot(_rmsnorm(x, gpre_ref[...]).astype(BF16), wq_ref[...]).astype(BF16)
    width = X_HEADS * X_HEAD_DIM
    outs = []
    for h in range(X_HEADS):
        cols = slice(h * X_HEAD_DIM, (h + 1) * X_HEAD_DIM)
        k = kv_ref[:, cols]
        v = kv_ref[:, width + h * X_HEAD_DIM: width + (h + 1) * X_HEAD_DIM]
        s = _dot_nt(q[:, cols], k)
        p = jnp.exp(s - jnp.max(s, axis=-1, keepdims=True))
        p = p / jnp.sum(p, axis=-1, keepdims=True)
        outs.append(_dot(p.astype(BF16), v).astype(BF16))
    o = jnp.concatenate(outs, axis=-1)
    o_ref[...] = x + _rmsnorm(_dot(o, wo_ref[...]), gpost_ref[...])


def _mix_cross_attention(a, w_mix, g_mix, x, g_pre, wq, kv, wo, g_post, *, tm=512):
    S, D = x.shape
    K = a.shape[1]
    full = lambda arr: pl.BlockSpec(arr.shape, lambda i: (0,) * arr.ndim, pipeline_mode=pl.Buffered(1))
    return pl.pallas_call(
        _mix_xattn_kernel,
        grid=(S // tm,),
        in_specs=[pl.BlockSpec((tm, K), lambda i: (i, 0)), full(w_mix), full(g_mix),
                  pl.BlockSpec((tm, D), lambda i: (i, 0)), full(g_pre), full(wq), full(kv), full(wo), full(g_post)],
        out_specs=pl.BlockSpec((tm, D), lambda i: (i, 0)),
        out_shape=jax.ShapeDtypeStruct((S, D), F32),
        compiler_params=_params("parallel"),
        name="mix_out_cross_attention",
    )(a, w_mix, g_mix, x, g_pre, wq, kv, wo, g_post)


def _swiglu_partial(hn, wg, wu, wd):
    a = _dot(hn, wg)
    u = _dot(hn, wu)
    return _dot((a * jax.nn.sigmoid(a) * u).astype(BF16), wd)


def _ffn_kernel(x_ref, gpre_ref, wg_ref, wu_ref, wd_ref, gpost_ref, o_ref, hn_ref, acc_ref):
    f = pl.program_id(1)

    @pl.when(f == 0)
    def _():
        hn_ref[...] = _rmsnorm(x_ref[...], gpre_ref[...]).astype(BF16)
        acc_ref[...] = jnp.zeros_like(acc_ref)

    acc_ref[...] += _swiglu_partial(hn_ref[...], wg_ref[...], wu_ref[...], wd_ref[...])

    @pl.when(f == pl.num_programs(1) - 1)
    def _():
        o_ref[...] = x_ref[...] + _rmsnorm(acc_ref[...], gpost_ref[...])


def _ffn(x, g_pre, wg, wu, wd, g_post, *, tm=512, tf=FFN_TF):
    S, D = x.shape
    F = wg.shape[1]
    return pl.pallas_call(
        _ffn_kernel,
        grid=(S // tm, F // tf),
        in_specs=[pl.BlockSpec((tm, D), lambda i, f: (i, 0)),
                  pl.BlockSpec((1, D), lambda i, f: (0, 0)),
                  pl.BlockSpec((D, tf), lambda i, f: (0, f)),
                  pl.BlockSpec((D, tf), lambda i, f: (0, f)),
                  pl.BlockSpec((tf, D), lambda i, f: (f, 0)),
                  pl.BlockSpec((1, D), lambda i, f: (0, 0))],
        out_specs=pl.BlockSpec((tm, D), lambda i, f: (i, 0)),
        out_shape=jax.ShapeDtypeStruct((S, D), F32),
        scratch_shapes=[pltpu.VMEM((tm, D), BF16), pltpu.VMEM((tm, D), F32)],
        compiler_params=_params("parallel", "arbitrary"),
        name="swiglu_ffn",
    )(x, g_pre, wg, wu, wd, g_post)


def _router_kernel(x_ref, g_ref, wr_ref, sel_ref, gate_ref):
    hn = _rmsnorm(x_ref[...], g_ref[...])
    logits = jnp.dot(hn, wr_ref[...], preferred_element_type=F32, precision=lax.Precision.HIGHEST)
    lane = lax.broadcasted_iota(jnp.int32, logits.shape, 1)
    neg_inf = -jnp.inf
    lg = jnp.where(lane < N_EXPERTS, logits, neg_inf)
    m1 = jnp.max(lg, axis=-1, keepdims=True)
    i1 = jnp.min(jnp.where(lg == m1, lane, LANES), axis=-1, keepdims=True)
    lg2 = jnp.where(lane == i1, neg_inf, lg)
    m2 = jnp.max(lg2, axis=-1, keepdims=True)
    i2 = jnp.min(jnp.where(lg2 == m2, lane, LANES), axis=-1, keepdims=True)
    e = jnp.exp(m2 - m1)
    g1 = 1.0 / (1.0 + e)
    sel_ref[...] = jnp.where(lane == 0, i1, jnp.where(lane == 1, i2, 0))
    gate_ref[...] = jnp.where(lane == 0, g1, jnp.where(lane == 1, e * g1, 0.0))


def _router(x, g_pre, w_router_padded, *, tm=512):
    S, D = x.shape
    return pl.pallas_call(
        _router_kernel,
        grid=(S // tm,),
        in_specs=[pl.BlockSpec((tm, D), lambda i: (i, 0)),
                  pl.BlockSpec((1, D), lambda i: (0, 0)),
                  pl.BlockSpec((D, LANES), lambda i: (0, 0))],
        out_specs=[pl.BlockSpec((tm, LANES), lambda i: (i, 0)),
                   pl.BlockSpec((tm, LANES), lambda i: (i, 0))],
        out_shape=[jax.ShapeDtypeStruct((S, LANES), jnp.int32),
                   jax.ShapeDtypeStruct((S, LANES), F32)],
        compiler_params=_params("parallel"),
        name="moe_router",
    )(x, g_pre, w_router_padded)


def _moe_plan(sel, gate, tg):
    S = sel.shape[0]
    P, E = 2 * S, N_EXPERTS
    i32 = jnp.int32
    e_flat = sel[:, :2].reshape(P)
    g_flat = gate[:, :2].reshape(P)
    onehot = (e_flat[:, None] == jnp.arange(E, dtype=i32)[None, :]).astype(i32)
    counts = jnp.sum(onehot, axis=0)
    rank = jnp.sum((jnp.cumsum(onehot, axis=0) - 1) * onehot, axis=1)
    ustart = jnp.cumsum(counts) - counts
    tiles = (counts + tg - 1) // tg
    tile_end = jnp.cumsum(tiles)
    tstart = tile_end - tiles
    n_tiles = P // tg + E
    n = jnp.arange(n_tiles, dtype=i32)
    tile_e = jnp.minimum(jnp.sum((n[:, None] >= tile_end[None, :]).astype(i32), axis=1), E - 1)
    used = n < tile_end[-1]
    local = ((n - tstart[tile_e]) * tg)[:, None] + jnp.arange(tg, dtype=i32)[None, :]
    valid = used[:, None] & (local < counts[tile_e][:, None])
    order = jnp.argsort(e_flat, stable=True).astype(i32)
    pair = order[jnp.clip(ustart[tile_e][:, None] + local, 0, P - 1)]
    src_token = jnp.where(valid, pair // 2, 0).reshape(-1)
    row_gate = jnp.where(valid, g_flat[pair], 0.0).reshape(-1, 1)
    dest = tstart[e_flat] * tg + rank
    return tile_e, used.astype(i32), src_token, row_gate, dest


def _row_gather(idx_ref, base, src_hbm, dst_ref, sem, n_rows, *, wait):
    for r in range(n_rows):
        cp = pltpu.make_async_copy(src_hbm.at[pl.ds(idx_ref[base + r], 1), :], dst_ref.at[pl.ds(r, 1), :], sem)
        if wait:
            cp.wait()
        else:
            cp.start()


def _moe_group_kernel(te_ref, used_ref, src_ref, x_hbm, gpre_ref, gate_ref, wg_ref, wu_ref, wd_ref, y_ref,
                      xg_ref, hb_ref, acc_ref, sem, *, tg):
    n = pl.program_id(0)
    f = pl.program_id(1)
    slot = n % 2

    @pl.when(f == 0)
    def _():
        @pl.when(n == 0)
        def _():
            _row_gather(src_ref, 0, x_hbm, xg_ref.at[0], sem.at[0], tg, wait=False)

        @pl.when(n + 1 < pl.num_programs(0))
        def _():
            _row_gather(src_ref, (n + 1) * tg, x_hbm, xg_ref.at[1 - slot], sem.at[1 - slot], tg, wait=False)

        _row_gather(src_ref, n * tg, x_hbm, xg_ref.at[slot], sem.at[slot], tg, wait=True)
        hb_ref[...] = _rmsnorm(xg_ref[slot], gpre_ref[...]).astype(BF16)
        acc_ref[...] = jnp.zeros_like(acc_ref)

    @pl.when(used_ref[n] > 0)
    def _():
        acc_ref[...] += _swiglu_partial(hb_ref[...], wg_ref[...], wu_ref[...], wd_ref[...])

    @pl.when(f == pl.num_programs(1) - 1)
    def _():
        y_ref[...] = acc_ref[...] * gate_ref[...]


def _moe_group(x, g_pre, tile_e, used, src_token, row_gate, wg, wu, wd, *, tg, tf=FFN_TF):
    S, D = x.shape
    E, _, F = wg.shape
    n_tiles = tile_e.shape[0]
    nf = F // tf
    fblk = lambda n, f, used: jnp.where(used[n] > 0, f, nf - 1)
    grid_spec = pltpu.PrefetchScalarGridSpec(
        num_scalar_prefetch=3,
        grid=(n_tiles, nf),
        in_specs=[pl.BlockSpec(memory_space=pl.ANY),
                  pl.BlockSpec((1, D), lambda n, f, te, used, src: (0, 0)),
                  pl.BlockSpec((tg, 1), lambda n, f, te, used, src: (n, 0)),
                  pl.BlockSpec((None, D, tf), lambda n, f, te, used, src: (te[n], 0, fblk(n, f, used))),
                  pl.BlockSpec((None, D, tf), lambda n, f, te, used, src: (te[n], 0, fblk(n, f, used))),
                  pl.BlockSpec((None, tf, D), lambda n, f, te, used, src: (te[n], fblk(n, f, used), 0))],
        out_specs=pl.BlockSpec((tg, D), lambda n, f, te, used, src: (n, 0)),
        scratch_shapes=[pltpu.VMEM((2, tg, D), F32), pltpu.VMEM((tg, D), BF16), pltpu.VMEM((tg, D), F32),
                        pltpu.SemaphoreType.DMA((2,))],
    )
    return pl.pallas_call(
        functools.partial(_moe_group_kernel, tg=tg),
        grid_spec=grid_spec,
        out_shape=jax.ShapeDtypeStruct((n_tiles * tg, D), F32),
        compiler_params=_params("arbitrary", "arbitrary"),
        name="moe_grouped_swiglu",
    )(tile_e, used, src_token, x, g_pre, row_gate, wg, wu, wd)


def _moe_combine_kernel(dest_ref, x_ref, y_hbm, g_ref, o_ref, yb_ref, sem, *, tm):
    i = pl.program_id(0)
    slot = i % 2

    def gather(tile, s, wait):
        base = tile * (2 * tm)
        for r in range(tm):
            for k in range(2):
                row = dest_ref[base + 2 * r + k]
                cp = pltpu.make_async_copy(y_hbm.at[pl.ds(row, 1), :], yb_ref.at[s, k, pl.ds(r, 1), :], sem.at[s])
                if wait:
                    cp.wait()
                else:
                    cp.start()

    @pl.when(i == 0)
    def _():
        gather(0, 0, False)

    @pl.when(i + 1 < pl.num_programs(0))
    def _():
        gather(i + 1, 1 - slot, False)

    gather(i, slot, True)
    o_ref[...] = x_ref[...] + _rmsnorm(yb_ref[slot, 0] + yb_ref[slot, 1], g_ref[...])


def _moe_combine(x, y, dest, g_post, *, tm=256):
    S, D = x.shape
    grid_spec = pltpu.PrefetchScalarGridSpec(
        num_scalar_prefetch=1,
        grid=(S // tm,),
        in_specs=[pl.BlockSpec((tm, D), lambda i, dest: (i, 0)),
                  pl.BlockSpec(memory_space=pl.ANY),
                  pl.BlockSpec((1, D), lambda i, dest: (0, 0))],
        out_specs=pl.BlockSpec((tm, D), lambda i, dest: (i, 0)),
        scratch_shapes=[pltpu.VMEM((2, 2, tm, D), F32), pltpu.SemaphoreType.DMA((2,))],
    )
    return pl.pallas_call(
        functools.partial(_moe_combine_kernel, tm=tm),
        grid_spec=grid_spec,
        out_shape=jax.ShapeDtypeStruct((S, D), F32),
        compiler_params=_params("arbitrary"),
        name="moe_combine",
    )(dest, x, y, g_post)


def _rope_tables(S):
    half = RET_D // 2
    inv = ROPE_BASE ** (-jnp.arange(half, dtype=F32) / half)
    ang = jnp.arange(S, dtype=jnp.int32)[:, None].astype(F32) * inv[None, :]
    cos, sin = jnp.cos(ang), jnp.sin(ang)
    return jnp.concatenate([cos, cos], axis=-1), jnp.concatenate([-sin, sin], axis=-1)


def _scaled_in_weights(w):
    ret_w = RET_HEADS * RET_D
    diff_col0 = 4 * ret_w + 3 * CONV_WIDTH
    diff_w = DIFF_HEADS * 2 * DIFF_D
    scale = jnp.ones((w.shape[1],), F32)
    scale = scale.at[:ret_w].set(RET_D ** -0.5)
    scale = scale.at[diff_col0:diff_col0 + diff_w].set(DIFF_D ** -0.5 * LOG2E)
    return (w * scale[None, :]).astype(BF16)


def kernel(x, mem, t5_bias, w_in, w_conv, diff_lambda, diff_subln, w_branch, w_mix_out, w_xq, w_xkv, w_xo, w_ffn_gate, w_ffn_up, w_ffn_down, w_router, w_exp_gate, w_exp_up, w_exp_down, g_pre_mix, g_post_mix, g_pre_xattn, g_mem, g_post_xattn, g_pre_ffn, g_post_ffn):
    B, S, D = x.shape
    assert B == 1
    depth = w_in.shape[0]
    ret_w = RET_HEADS * RET_D
    conv_col0 = 4 * ret_w
    diff_col0 = conv_col0 + 3 * CONV_WIDTH
    gate_col0 = diff_col0 + 3 * DIFF_HEADS * 2 * DIFF_D
    cos2, sin2 = _rope_tables(S)
    near_bias = _near_bias_tiles(t5_bias, ATTN_BLOCK)
    xs = x[0]
    mem2 = mem[0]
    row = lambda g: g[None, :]
    for l in range(depth):
        proj = _in_projection(xs, row(g_pre_mix[l]), _scaled_in_weights(w_in[l]))
        ya = _retention(proj, cos2, sin2)
        lam_init = 0.8 - 0.6 * math.exp(-0.3 * l)
        yc = _diff_attention(proj, near_bias, t5_bias, diff_lambda[l], row(diff_subln[l]), lam_init,
                             col0=diff_col0, t=ATTN_BLOCK)
        merged = _merge(ya, yc, proj, w_conv[l], w_branch[l].astype(BF16),
                        conv_col0=conv_col0, gate_col0=gate_col0)
        kv = _mem_kv(mem2, row(g_mem[l]), w_xkv[l].astype(BF16))
        xs = _mix_cross_attention(merged, w_mix_out[l].astype(BF16), row(g_post_mix[l]), xs,
                                  row(g_pre_xattn[l]), (w_xq[l] * X_HEAD_DIM ** -0.5).astype(BF16), kv,
                                  w_xo[l].astype(BF16), row(g_post_xattn[l]))
        if l % 2 == 0:
            xs = _ffn(xs, row(g_pre_ffn[l]), w_ffn_gate[l // 2].astype(BF16), w_ffn_up[l // 2].astype(BF16),
                      w_ffn_down[l // 2].astype(BF16), row(g_post_ffn[l]))
        else:
            wr = jnp.pad(w_router[l // 2], ((0, 0), (0, LANES - N_EXPERTS)))
            sel, gate = _router(xs, row(g_pre_ffn[l]), wr)
            tile_e, used, src_token, row_gate, dest = _moe_plan(sel, gate, MOE_TILE)
            y = _moe_group(xs, row(g_pre_ffn[l]), tile_e, used, src_token, row_gate, w_exp_gate[l // 2].astype(BF16),
                           w_exp_up[l // 2].astype(BF16), w_exp_down[l // 2].astype(BF16), tg=MOE_TILE)
            xs = _moe_combine(xs, y, dest, row(g_post_ffn[l]))
    return xs[None]
```

```python
import functools
import math

import numpy as np
import jax
import jax.numpy as jnp
from jax import lax
from jax.experimental import pallas as pl
from jax.experimental.pallas import tpu as pltpu

F32 = jnp.float32
BF16 = jnp.bfloat16

EPS = 1e-6
RET_HEADS = 8
RET_D = 128
RET_CHUNK = 128
ROPE_BASE = 10000.0
CONV_WIDTH = 1024
CONV_K = 3
DIFF_HEADS = 4
DIFF_D = 128
T5_BUCKETS = 32
T5_MAX_DIST = 128
X_HEADS = 4
X_HEAD_DIM = 128
N_EXPERTS = 8
N_BRANCH = 3
BRANCH_W = 1024

V7X_VMEM_BYTES = 64 * 1024 * 1024
VMEM_LIMIT = V7X_VMEM_BYTES - 8 * 1024 * 1024
LANES = 128
NEG_BIG = -1e30
LOG2E = math.log2(math.e)
ATTN_BLOCK = 512
FAR_UNROLL = 4
MOE_TILE = 512
FFN_TF = 1024


def _params(*semantics):
    return pltpu.CompilerParams(dimension_semantics=semantics, vmem_limit_bytes=VMEM_LIMIT)


def _rmsnorm(x, g=None):
    y = x * lax.rsqrt(jnp.mean(x * x, axis=-1, keepdims=True) + EPS)
    return y if g is None else y * g


def _dot(a, b):
    return jnp.dot(a, b, preferred_element_type=F32)


def _dot_nt(a, b):
    return lax.dot_general(a, b, (((1,), (1,)), ((), ())), preferred_element_type=F32)


def _cast_scratch(rows, cols):
    return [pltpu.VMEM((2, rows, cols), F32), pltpu.VMEM((rows, cols), BF16),
            pltpu.SemaphoreType.DMA((2,)), pltpu.SemaphoreType.DMA((1,))]


def _cast_step(step, n_steps, src_hbm, dst_hbm, in_buf, out_buf, in_sem, out_sem):
    rows = out_buf.shape[0]

    def chunk(c):
        return pl.ds(pl.multiple_of(c * rows, rows), rows)

    def in_copy(c, slot):
        return pltpu.make_async_copy(src_hbm.at[chunk(c), :], in_buf.at[slot], in_sem.at[slot])

    def out_copy(c):
        return pltpu.make_async_copy(out_buf, dst_hbm.at[chunk(c), :], out_sem.at[0])

    slot = step % 2

    @pl.when(step == 0)
    def _():
        in_copy(0, 0).start()

    @pl.when(step + 1 < n_steps)
    def _():
        in_copy(step + 1, 1 - slot).start()

    in_copy(step, slot).wait()

    @pl.when(step >= 1)
    def _():
        out_copy(step - 1).wait()

    out_buf[...] = in_buf[slot].astype(BF16)
    out_copy(step).start()

    @pl.when(step == n_steps - 1)
    def _():
        out_copy(step).wait()


def _cast_specs(src, n_steps):
    rows_total, cols = src.shape
    assert rows_total % n_steps == 0
    any_spec = pl.BlockSpec(memory_space=pl.ANY)
    return any_spec, any_spec, jax.ShapeDtypeStruct(src.shape, BF16), _cast_scratch(rows_total // n_steps, cols)


def _in_proj_kernel(x_ref, g_ref, w_ref, *rest, hosts_cast):
    if hosts_cast:
        src_hbm, o_ref, dst_hbm, hn_ref, *cast_scratch = rest
        step = pl.program_id(0) * pl.num_programs(1) + pl.program_id(1)
        _cast_step(step, pl.num_programs(0) * pl.num_programs(1), src_hbm, dst_hbm, *cast_scratch)
    else:
        o_ref, hn_ref = rest

    @pl.when(pl.program_id(1) == 0)
    def _():
        hn_ref[...] = _rmsnorm(x_ref[...], g_ref[...]).astype(BF16)

    o_ref[...] = _dot(hn_ref[...], w_ref[...]).astype(BF16)


def _in_projection(x, g, w, cast_src=None, *, tm=1024):
    S, D = x.shape
    N = w.shape[1]
    tn = 2048 if cast_src is None else 1024
    grid = (S // tm, N // tn)
    in_specs = [pl.BlockSpec((tm, D), lambda i, j: (i, 0)),
                pl.BlockSpec((1, D), lambda i, j: (0, 0)),
                pl.BlockSpec((D, tn), lambda i, j: (0, j))]
    out_specs = [pl.BlockSpec((tm, tn), lambda i, j: (i, j))]
    out_shape = [jax.ShapeDtypeStruct((S, N), BF16)]
    scratch = [pltpu.VMEM((tm, D), BF16)]
    args = [x, g, w]
    if cast_src is not None:
        in_spec, out_spec, shape, cast_scratch = _cast_specs(cast_src, grid[0] * grid[1])
        in_specs.append(in_spec)
        out_specs.append(out_spec)
        out_shape.append(shape)
        scratch += cast_scratch
        args.append(cast_src)
    outs = pl.pallas_call(
        functools.partial(_in_proj_kernel, hosts_cast=cast_src is not None),
        grid=grid,
        in_specs=in_specs,
        out_specs=out_specs,
        out_shape=out_shape,
        scratch_shapes=scratch,
        compiler_params=_params("arbitrary", "arbitrary"),
        name="in_projection",
    )(*args)
    return outs[0] if cast_src is None else tuple(outs)


def _retention_consts():
    C, H = RET_CHUNK, RET_HEADS
    log_g = np.log1p(-(2.0 ** (-5.0 - np.arange(H, dtype=np.float64))))
    idx = np.arange(C, dtype=np.float64)
    dist = idx[:, None] - idx[None, :]
    intra = np.where(dist[None] >= 0, np.exp(np.maximum(dist, 0.0)[None] * log_g[:, None, None]), 0.0)
    kdec = np.exp((C - 1 - idx)[None, :, None] * log_g[:, None, None]) * np.ones((1, 1, RET_D))
    qdec = np.exp((idx + 1.0)[None, :, None] * log_g[:, None, None]) * np.ones((1, 1, RET_D))
    chunk_decay = tuple(float(v) for v in np.exp(C * log_g))
    return (jnp.asarray(intra, F32), jnp.asarray(qdec, F32), jnp.asarray(kdec, F32), chunk_decay)


def _rope(x, cos2, sin2):
    return x * cos2 + pltpu.roll(x, RET_D // 2, 1) * sin2


def _retention_kernel(q_ref, k_ref, v_ref, g_ref, cos_ref, sin_ref, intra_ref, qdec_ref, kdec_ref,
                      o_ref, state_ref, *, chunks, chunk_decay):
    @pl.when(pl.program_id(0) == 0)
    def _():
        state_ref[...] = jnp.zeros_like(state_ref)

    C = RET_CHUNK
    for c in range(chunks):
        rows = slice(c * C, (c + 1) * C)
        cos2 = cos_ref[rows, :]
        sin2 = sin_ref[rows, :]
        for h in range(RET_HEADS):
            cols = slice(h * RET_D, (h + 1) * RET_D)
            q = _rope(q_ref[rows, cols].astype(F32), cos2, sin2)
            k = _rope(k_ref[rows, cols].astype(F32), cos2, sin2)
            v = v_ref[rows, cols]
            state = state_ref[h]
            scores = _dot_nt(q.astype(BF16), k.astype(BF16)) * intra_ref[h]
            o = _dot(scores.astype(BF16), v) + _dot((q * qdec_ref[h]).astype(BF16), state.astype(BF16))
            kv = _dot((k * kdec_ref[h]).T.astype(BF16), v)
            state_ref[h] = state * chunk_decay[h] + kv
            gate = g_ref[rows, cols].astype(F32)
            o_ref[rows, cols] = (gate * jax.nn.sigmoid(gate) * _rmsnorm(o)).astype(BF16)


def _retention(proj, cos2, sin2, *, chunks=2):
    S = proj.shape[0]
    W = RET_HEADS * RET_D
    R = chunks * RET_CHUNK
    intra, qdec, kdec, chunk_decay = _retention_consts()
    col = lambda b: pl.BlockSpec((R, W), lambda i, b=b: (i, b))
    const = pl.BlockSpec((RET_HEADS, RET_CHUNK, RET_D), lambda i: (0, 0, 0))
    rope_spec = pl.BlockSpec((R, RET_D), lambda i: (i, 0))
    return pl.pallas_call(
        functools.partial(_retention_kernel, chunks=chunks, chunk_decay=chunk_decay),
        grid=(S // R,),
        in_specs=[col(0), col(1), col(2), col(3), rope_spec, rope_spec, const, const, const],
        out_specs=pl.BlockSpec((R, W), lambda i: (i, 0)),
        out_shape=jax.ShapeDtypeStruct((S, W), BF16),
        scratch_shapes=[pltpu.VMEM((RET_HEADS, RET_D, RET_D), F32)],
        compiler_params=_params("arbitrary"),
        name="retention",
    )(proj, proj, proj, proj, cos2, sin2, intra, qdec, kdec)


def _t5_bucket(dist):
    max_exact = T5_BUCKETS // 2
    d = jnp.maximum(dist, 0)
    df = jnp.maximum(d, 1).astype(F32)
    large = max_exact + (jnp.log(df / max_exact) / math.log(T5_MAX_DIST / max_exact)
                         * (T5_BUCKETS - max_exact)).astype(jnp.int32)
    large = jnp.minimum(large, T5_BUCKETS - 1)
    return jnp.where(d < max_exact, d, large)


def _near_bias_tiles(t5_bias, t):
    assert t >= T5_MAX_DIST
    ql = jnp.arange(t, dtype=jnp.int32)[:, None]
    kl = jnp.arange(t, dtype=jnp.int32)[None, :]
    tab = t5_bias.T.astype(F32) * LOG2E
    tiles = []
    for off in (0, 1):
        dist = ql + off * t - kl
        bucket = _t5_bucket(dist)[None]
        b = jnp.zeros((tab.shape[0], t, t), F32)
        for n in range(T5_BUCKETS):
            b = jnp.where(bucket == n, tab[:, n, None, None], b)
        tiles.append(jnp.where(dist[None] >= 0, b, NEG_BIG))
    return jnp.stack(tiles, axis=1)


def _diff_attn_kernel(t5_ref, lam_ref, q_ref, k_ref, v_ref, bias_ref, subg_ref, *rest, t, lam_init, hosts_cast):
    h = pl.program_id(0)
    i = pl.program_id(1)
    if hosts_cast:
        src_hbm, o_ref, dst_hbm, mx_ref, l_ref, acc_ref, *cast_scratch = rest
        _cast_step(h * pl.num_programs(1) + i, pl.num_programs(0) * pl.num_programs(1),
                   src_hbm, dst_hbm, *cast_scratch)
    else:
        o_ref, mx_ref, l_ref, acc_ref = rest
    far_bias = t5_ref[h, T5_BUCKETS - 1] * LOG2E
    q = q_ref[...]

    def key_block(j):
        return pl.ds(pl.multiple_of(j * t, t), t)

    def online_block(j, bias, shift):
        k = k_ref[key_block(j), :]
        v = v_ref[key_block(j), :]
        for mp in range(2):
            cols = slice(mp * DIFF_D, (mp + 1) * DIFF_D)
            s = _dot_nt(q[:, cols], k[:, cols])
            if bias is not None:
                s = s + bias
            groups = [s[:, g * LANES:(g + 1) * LANES] for g in range(t // LANES)]
            bm = functools.reduce(jnp.maximum, groups)
            m_old = mx_ref[mp]
            m_new = jnp.maximum(m_old, jnp.max(bm, axis=-1, keepdims=True) + shift)
            alpha = jnp.exp2(m_old - m_new)
            ms = m_new - shift
            ps = [jnp.exp2(g - ms) for g in groups]
            l_ref[mp] = alpha * l_ref[mp] + functools.reduce(lambda a, b: a + b, ps)
            acc_ref[mp] = (jnp.concatenate([alpha, alpha], axis=-1) * acc_ref[mp]
                           + _dot(jnp.concatenate(ps, axis=-1).astype(BF16), v))
            mx_ref[mp] = m_new

    mx_ref[...] = jnp.full_like(mx_ref, NEG_BIG)
    l_ref[...] = jnp.zeros_like(l_ref)
    acc_ref[...] = jnp.zeros_like(acc_ref)

    n = jnp.maximum(i - 1, 0)

    def far_body(jj, carry):
        for u in range(FAR_UNROLL):
            online_block(jj * FAR_UNROLL + u, None, far_bias)
        return carry

    lax.fori_loop(0, n // FAR_UNROLL, far_body, 0)
    for r in range(1, FAR_UNROLL):
        @pl.when(n % FAR_UNROLL == r)
        def _():
            for u in range(r):
                online_block(n - r + u, None, far_bias)

    @pl.when(i >= 1)
    def _():
        online_block(i - 1, bias_ref[1], 0.0)
        online_block(i, bias_ref[0], 0.0)

    @pl.when(i == 0)
    def _():
        online_block(i, bias_ref[0], 0.0)

    lam_p = lam_ref[...]
    lam = (jnp.exp(jnp.sum(lam_p[0:1] * lam_p[1:2], axis=-1, keepdims=True))
           - jnp.exp(jnp.sum(lam_p[2:3] * lam_p[3:4], axis=-1, keepdims=True)) + lam_init)
    l0 = jnp.sum(l_ref[0], axis=-1, keepdims=True)
    l1 = jnp.sum(l_ref[1], axis=-1, keepdims=True)
    o = acc_ref[0] / l0 - lam * (acc_ref[1] / l1)
    o_ref[...] = (_rmsnorm(o, subg_ref[...]) * (1.0 - lam_init)).astype(BF16)


def _diff_attention(proj, bias, t5_bias, lam_params, subln_g, lam_init, cast_src=None, *, col0, t):
    S = proj.shape[0]
    HW = 2 * DIFF_D
    qb = col0 // HW
    kb = qb + DIFF_HEADS
    vb = kb + DIFF_HEADS
    grid = (DIFF_HEADS, S // t)
    in_specs = [pl.BlockSpec(memory_space=pltpu.SMEM),
                pl.BlockSpec((4, DIFF_D), lambda h, i: (0, 0)),
                pl.BlockSpec((t, HW), lambda h, i: (i, qb + h)),
                pl.BlockSpec((S, HW), lambda h, i: (0, kb + h)),
                pl.BlockSpec((S, HW), lambda h, i: (0, vb + h)),
                pl.BlockSpec((None, 2, t, t), lambda h, i: (h, 0, 0, 0), pipeline_mode=pl.Buffered(1)),
                pl.BlockSpec((1, HW), lambda h, i: (0, 0))]
    out_specs = [pl.BlockSpec((t, HW), lambda h, i: (i, h))]
    out_shape = [jax.ShapeDtypeStruct((S, DIFF_HEADS * HW), BF16)]
    scratch = [pltpu.VMEM((2, t, LANES), F32), pltpu.VMEM((2, t, LANES), F32), pltpu.VMEM((2, t, HW), F32)]
    args = [t5_bias.T.astype(F32), lam_params, proj, proj, proj, bias, subln_g]
    if cast_src is not None:
        in_spec, out_spec, shape, cast_scratch = _cast_specs(cast_src, grid[0] * grid[1])
        in_specs.append(in_spec)
        out_specs.append(out_spec)
        out_shape.append(shape)
        scratch += cast_scratch
        args.append(cast_src)
    outs = pl.pallas_call(
        functools.partial(_diff_attn_kernel, t=t, lam_init=lam_init, hosts_cast=cast_src is not None),
        grid=grid,
        in_specs=in_specs,
        out_specs=out_specs,
        out_shape=out_shape,
        scratch_shapes=scratch,
        compiler_params=_params("arbitrary", "arbitrary"),
        name="diff_attention",
    )(*args)
    return outs[0] if cast_src is None else tuple(outs)


def _merge_kernel(ya_ref, cb_ref, cc_ref, cu_ref, hc_ref, hu_ref, wconv_ref, yc_ref,
                  g0_ref, g1_ref, g2_ref, wb_ref, o_ref, *, halo):
    i = pl.program_id(0)
    v = cc_ref[...].astype(F32) * cu_ref[...].astype(F32)
    hv = jnp.where(i > 0, hc_ref[...].astype(F32) * hu_ref[...].astype(F32), 0.0)
    prev1 = hv[halo - 1:halo, :]
    prev2 = hv[halo - 2:halo - 1, :]
    row = lax.broadcasted_iota(jnp.int32, v.shape, 0)
    v1 = jnp.where(row == 0, prev1, pltpu.roll(v, 1, 0))
    v2 = jnp.where(row == 0, prev2, jnp.where(row == 1, prev1, pltpu.roll(v, 2, 0)))
    w = wconv_ref[...]
    yb = cb_ref[...].astype(F32) * (w[0:1] * v2 + w[1:2] * v1 + w[2:3] * v)
    merged = jax.nn.sigmoid(g0_ref[...].astype(F32)) * _dot(ya_ref[...], wb_ref[0])
    merged += jax.nn.sigmoid(g1_ref[...].astype(F32)) * _dot(yb.astype(BF16), wb_ref[1])
    merged += jax.nn.sigmoid(g2_ref[...].astype(F32)) * _dot(yc_ref[...], wb_ref[2])
    o_ref[...] = merged.astype(BF16)


def _merge(ya, yc, proj, w_conv, w_branch, *, conv_col0, gate_col0, tm=256, halo=16):
    S = proj.shape[0]
    W = BRANCH_W
    D = w_branch.shape[-1]
    cblk = conv_col0 // W
    gblk = gate_col0 // D
    hb = tm // halo
    row = lambda b: pl.BlockSpec((tm, W), lambda i, b=b: (i, b))
    halo_spec = lambda b: pl.BlockSpec((halo, W), lambda i, b=b: (jnp.maximum(i * hb - 1, 0), b))
    gate = lambda b: pl.BlockSpec((tm, D), lambda i, b=b: (i, b))
    return pl.pallas_call(
        functools.partial(_merge_kernel, halo=halo),
        grid=(S // tm,),
        in_specs=[pl.BlockSpec((tm, W), lambda i: (i, 0)),
                  row(cblk), row(cblk + 1), row(cblk + 2), halo_spec(cblk + 1), halo_spec(cblk + 2),
                  pl.BlockSpec((CONV_K, W), lambda i: (0, 0)),
                  pl.BlockSpec((tm, W), lambda i: (i, 0)),
                  gate(gblk), gate(gblk + 1), gate(gblk + 2),
                  pl.BlockSpec((N_BRANCH, W, D), lambda i: (0, 0, 0))],
        out_specs=pl.BlockSpec((tm, D), lambda i: (i, 0)),
        out_shape=jax.ShapeDtypeStruct((S, D), BF16),
        compiler_params=_params("parallel"),
        name="conv_gated_merge",
    )(ya, proj, proj, proj, proj, proj, w_conv, yc, proj, proj, proj, w_branch)


def _mem_kv_kernel(mem_ref, g_ref, w_ref, o_ref):
    o_ref[...] = _dot(_rmsnorm(mem_ref[...], g_ref[...]).astype(BF16), w_ref[...]).astype(BF16)


def _mem_kv(mem, g, wkv):
    M, D = mem.shape
    N = wkv.shape[1]
    return pl.pallas_call(
        _mem_kv_kernel,
        out_shape=jax.ShapeDtypeStruct((M, N), BF16),
        compiler_params=pltpu.CompilerParams(vmem_limit_bytes=VMEM_LIMIT),
        name="memory_kv",
    )(mem, g, wkv)


def _mix_xattn_kernel(a_ref, wmix_ref, gmix_ref, x_ref, gpre_ref, wq_ref, kv_ref, wo_ref, gpost_ref, o_ref):
    x = x_ref[...] + _rmsnorm(_dot(a_ref[...], wmix_ref[...]), gmix_ref[...])
    q = _dot(_rmsnorm(x, gpre_ref[...]).astype(BF16), wq_ref[...]).astype(BF16)
    width = X_HEADS * X_HEAD_DIM
    outs = []
    for h in range(X_HEADS):
        cols = slice(h * X_HEAD_DIM, (h + 1) * X_HEAD_DIM)
        k = kv_ref[:, cols]
        v = kv_ref[:, width + h * X_HEAD_DIM: width + (h + 1) * X_HEAD_DIM]
        s = _dot_nt(q[:, cols], k)
        p = jnp.exp(s - jnp.max(s, axis=-1, keepdims=True))
        p = p / jnp.sum(p, axis=-1, keepdims=True)
        outs.append(_dot(p.astype(BF16), v).astype(BF16))
    o = jnp.concatenate(outs, axis=-1)
    o_ref[...] = x + _rmsnorm(_dot(o, wo_ref[...]), gpost_ref[...])


def _mix_cross_attention(a, w_mix, g_mix, x, g_pre, wq, kv, wo, g_post, *, tm=512):
    S, D = x.shape
    K = a.shape[1]
    full = lambda arr: pl.BlockSpec(arr.shape, lambda i: (0,) * arr.ndim, pipeline_mode=pl.Buffered(1))
    return pl.pallas_call(
        _mix_xattn_kernel,
        grid=(S // tm,),
        in_specs=[pl.BlockSpec((tm, K), lambda i: (i, 0)), full(w_mix), full(g_mix),
                  pl.BlockSpec((tm, D), lambda i: (i, 0)), full(g_pre), full(wq), full(kv), full(wo), full(g_post)],
        out_specs=pl.BlockSpec((tm, D), lambda i: (i, 0)),
        out_shape=jax.ShapeDtypeStruct((S, D), F32),
        compiler_params=_params("parallel"),
        name="mix_out_cross_attention",
    )(a, w_mix, g_mix, x, g_pre, wq, kv, wo, g_post)


def _swiglu_partial(hn, wg, wu, wd):
    a = _dot(hn, wg)
    u = _dot(hn, wu)
    return _dot((a * jax.nn.sigmoid(a) * u).astype(BF16), wd)


def _ffn_kernel(x_ref, gpre_ref, wg_ref, wu_ref, wd_ref, gpost_ref, o_ref, hn_ref, acc_ref):
    f = pl.program_id(1)

    @pl.when(f == 0)
    def _():
        hn_ref[...] = _rmsnorm(x_ref[...], gpre_ref[...]).astype(BF16)
        acc_ref[...] = jnp.zeros_like(acc_ref)

    acc_ref[...] += _swiglu_partial(hn_ref[...], wg_ref[...], wu_ref[...], wd_ref[...])

    @pl.when(f == pl.num_programs(1) - 1)
    def _():
        o_ref[...] = x_ref[...] + _rmsnorm(acc_ref[...], gpost_ref[...])


def _ffn(x, g_pre, wg, wu, wd, g_post, *, tm=512, tf=FFN_TF):
    S, D = x.shape
    F = wg.shape[1]
    return pl.pallas_call(
        _ffn_kernel,
        grid=(S // tm, F // tf),
        in_specs=[pl.BlockSpec((tm, D), lambda i, f: (i, 0)),
                  pl.BlockSpec((1, D), lambda i, f: (0, 0)),
                  pl.BlockSpec((D, tf), lambda i, f: (0, f)),
                  pl.BlockSpec((D, tf), lambda i, f: (0, f)),
                  pl.BlockSpec((tf, D), lambda i, f: (f, 0)),
                  pl.BlockSpec((1, D), lambda i, f: (0, 0))],
        out_specs=pl.BlockSpec((tm, D), lambda i, f: (i, 0)),
        out_shape=jax.ShapeDtypeStruct((S, D), F32),
        scratch_shapes=[pltpu.VMEM((tm, D), BF16), pltpu.VMEM((tm, D), F32)],
        compiler_params=_params("parallel", "arbitrary"),
        name="swiglu_ffn",
    )(x, g_pre, wg, wu, wd, g_post)


def _router_kernel(x_ref, g_ref, wr_ref, sel_ref, gate_ref):
    hn = _rmsnorm(x_ref[...], g_ref[...])
    logits = jnp.dot(hn, wr_ref[...], preferred_element_type=F32, precision=lax.Precision.HIGHEST)
    lane = lax.broadcasted_iota(jnp.int32, logits.shape, 1)
    neg_inf = -jnp.inf
    lg = jnp.where(lane < N_EXPERTS, logits, neg_inf)
    m1 = jnp.max(lg, axis=-1, keepdims=True)
    i1 = jnp.min(jnp.where(lg == m1, lane, LANES), axis=-1, keepdims=True)
    lg2 = jnp.where(lane == i1, neg_inf, lg)
    m2 = jnp.max(lg2, axis=-1, keepdims=True)
    i2 = jnp.min(jnp.where(lg2 == m2, lane, LANES), axis=-1, keepdims=True)
    e = jnp.exp(m2 - m1)
    g1 = 1.0 / (1.0 + e)
    sel_ref[...] = jnp.where(lane == 0, i1, jnp.where(lane == 1, i2, 0))
    gate_ref[...] = jnp.where(lane == 0, g1, jnp.where(lane == 1, e * g1, 0.0))


def _router(x, g_pre, w_router_padded, *, tm=512):
    S, D = x.shape
    return pl.pallas_call(
        _router_kernel,
        grid=(S // tm,),
        in_specs=[pl.BlockSpec((tm, D), lambda i: (i, 0)),
                  pl.BlockSpec((1, D), lambda i: (0, 0)),
                  pl.BlockSpec((D, LANES), lambda i: (0, 0))],
        out_specs=[pl.BlockSpec((tm, LANES), lambda i: (i, 0)),
                   pl.BlockSpec((tm, LANES), lambda i: (i, 0))],
        out_shape=[jax.ShapeDtypeStruct((S, LANES), jnp.int32),
                   jax.ShapeDtypeStruct((S, LANES), F32)],
        compiler_params=_params("parallel"),
        name="moe_router",
    )(x, g_pre, w_router_padded)


def _moe_plan(sel, gate, tg):
    S = sel.shape[0]
    P, E = 2 * S, N_EXPERTS
    i32 = jnp.int32
    e_flat = sel[:, :2].reshape(P)
    g_flat = gate[:, :2].reshape(P)
    onehot = (e_flat[:, None] == jnp.arange(E, dtype=i32)[None, :]).astype(i32)
    counts = jnp.sum(onehot, axis=0)
    rank = jnp.sum((jnp.cumsum(onehot, axis=0) - 1) * onehot, axis=1)
    ustart = jnp.cumsum(counts) - counts
    tiles = (counts + tg - 1) // tg
    tile_end = jnp.cumsum(tiles)
    tstart = tile_end - tiles
    n_tiles = P // tg + E
    n = jnp.arange(n_tiles, dtype=i32)
    tile_e = jnp.minimum(jnp.sum((n[:, None] >= tile_end[None, :]).astype(i32), axis=1), E - 1)
    used = n < tile_end[-1]
    local = ((n - tstart[tile_e]) * tg)[:, None] + jnp.arange(tg, dtype=i32)[None, :]
    valid = used[:, None] & (local < counts[tile_e][:, None])
    order = jnp.argsort(e_flat, stable=True).astype(i32)
    pair = order[jnp.clip(ustart[tile_e][:, None] + local, 0, P - 1)]
    src_token = jnp.where(valid, pair // 2, 0).reshape(-1)
    row_gate = jnp.where(valid, g_flat[pair], 0.0).reshape(-1, 1)
    dest = tstart[e_flat] * tg + rank
    return tile_e, used.astype(i32), src_token, row_gate, dest


def _row_gather(idx_ref, base, src_hbm, dst_ref, sem, n_rows, *, wait):
    for r in range(n_rows):
        cp = pltpu.make_async_copy(src_hbm.at[pl.ds(idx_ref[base + r], 1), :], dst_ref.at[pl.ds(r, 1), :], sem)
        if wait:
            cp.wait()
        else:
            cp.start()


def _moe_group_kernel(te_ref, used_ref, src_ref, x_hbm, gpre_ref, gate_ref, wg_ref, wu_ref, wd_ref, y_ref,
                      xg_ref, hb_ref, acc_ref, sem, *, tg):
    n = pl.program_id(0)
    f = pl.program_id(1)
    slot = n % 2

    @pl.when(f == 0)
    def _():
        @pl.when(n == 0)
        def _():
            _row_gather(src_ref, 0, x_hbm, xg_ref.at[0], sem.at[0], tg, wait=False)

        @pl.when(n + 1 < pl.num_programs(0))
        def _():
            _row_gather(src_ref, (n + 1) * tg, x_hbm, xg_ref.at[1 - slot], sem.at[1 - slot], tg, wait=False)

        _row_gather(src_ref, n * tg, x_hbm, xg_ref.at[slot], sem.at[slot], tg, wait=True)
        hb_ref[...] = _rmsnorm(xg_ref[slot], gpre_ref[...]).astype(BF16)
        acc_ref[...] = jnp.zeros_like(acc_ref)

    @pl.when(used_ref[n] > 0)
    def _():
        acc_ref[...] += _swiglu_partial(hb_ref[...], wg_ref[...], wu_ref[...], wd_ref[...])

    @pl.when(f == pl.num_programs(1) - 1)
    def _():
        y_ref[...] = acc_ref[...] * gate_ref[...]


def _moe_group(x, g_pre, tile_e, used, src_token, row_gate, wg, wu, wd, *, tg, tf=FFN_TF):
    S, D = x.shape
    E, _, F = wg.shape
    n_tiles = tile_e.shape[0]
    nf = F // tf
    fblk = lambda n, f, used: jnp.where(used[n] > 0, f, nf - 1)
    grid_spec = pltpu.PrefetchScalarGridSpec(
        num_scalar_prefetch=3,
        grid=(n_tiles, nf),
        in_specs=[pl.BlockSpec(memory_space=pl.ANY),
                  pl.BlockSpec((1, D), lambda n, f, te, used, src: (0, 0)),
                  pl.BlockSpec((tg, 1), lambda n, f, te, used, src: (n, 0)),
                  pl.BlockSpec((None, D, tf), lambda n, f, te, used, src: (te[n], 0, fblk(n, f, used))),
                  pl.BlockSpec((None, D, tf), lambda n, f, te, used, src: (te[n], 0, fblk(n, f, used))),
                  pl.BlockSpec((None, tf, D), lambda n, f, te, used, src: (te[n], fblk(n, f, used), 0))],
        out_specs=pl.BlockSpec((tg, D), lambda n, f, te, used, src: (n, 0)),
        scratch_shapes=[pltpu.VMEM((2, tg, D), F32), pltpu.VMEM((tg, D), BF16), pltpu.VMEM((tg, D), F32),
                        pltpu.SemaphoreType.DMA((2,))],
    )
    return pl.pallas_call(
        functools.partial(_moe_group_kernel, tg=tg),
        grid_spec=grid_spec,
        out_shape=jax.ShapeDtypeStruct((n_tiles * tg, D), F32),
        compiler_params=_params("arbitrary", "arbitrary"),
        name="moe_grouped_swiglu",
    )(tile_e, used, src_token, x, g_pre, row_gate, wg, wu, wd)


def _moe_combine_kernel(dest_ref, x_ref, y_hbm, g_ref, o_ref, yb_ref, sem, *, tm):
    i = pl.program_id(0)
    slot = i % 2

    def gather(tile, s, wait):
        base = tile * (2 * tm)
        for r in range(tm):
            for k in range(2):
                row = dest_ref[base + 2 * r + k]
                cp = pltpu.make_async_copy(y_hbm.at[pl.ds(row, 1), :], yb_ref.at[s, k, pl.ds(r, 1), :], sem.at[s])
                if wait:
                    cp.wait()
                else:
                    cp.start()

    @pl.when(i == 0)
    def _():
        gather(0, 0, False)

    @pl.when(i + 1 < pl.num_programs(0))
    def _():
        gather(i + 1, 1 - slot, False)

    gather(i, slot, True)
    o_ref[...] = x_ref[...] + _rmsnorm(yb_ref[slot, 0] + yb_ref[slot, 1], g_ref[...])


def _moe_combine(x, y, dest, g_post, *, tm=256):
    S, D = x.shape
    grid_spec = pltpu.PrefetchScalarGridSpec(
        num_scalar_prefetch=1,
        grid=(S // tm,),
        in_specs=[pl.BlockSpec((tm, D), lambda i, dest: (i, 0)),
                  pl.BlockSpec(memory_space=pl.ANY),
                  pl.BlockSpec((1, D), lambda i, dest: (0, 0))],
        out_specs=pl.BlockSpec((tm, D), lambda i, dest: (i, 0)),
        scratch_shapes=[pltpu.VMEM((2, 2, tm, D), F32), pltpu.SemaphoreType.DMA((2,))],
    )
    return pl.pallas_call(
        functools.partial(_moe_combine_kernel, tm=tm),
        grid_spec=grid_spec,
        out_shape=jax.ShapeDtypeStruct((S, D), F32),
        compiler_params=_params("arbitrary"),
        name="moe_combine",
    )(dest, x, y, g_post)


def _rope_tables(S):
    half = RET_D // 2
    inv = ROPE_BASE ** (-jnp.arange(half, dtype=F32) / half)
    ang = jnp.arange(S, dtype=jnp.int32)[:, None].astype(F32) * inv[None, :]
    cos, sin = jnp.cos(ang), jnp.sin(ang)
    return jnp.concatenate([cos, cos], axis=-1), jnp.concatenate([-sin, sin], axis=-1)


def _scaled_in_weights(w):
    ret_w = RET_HEADS * RET_D
    diff_col0 = 4 * ret_w + 3 * CONV_WIDTH
    diff_w = DIFF_HEADS * 2 * DIFF_D
    scale = jnp.ones((w.shape[1],), F32)
    scale = scale.at[:ret_w].set(RET_D ** -0.5)
    scale = scale.at[diff_col0:diff_col0 + diff_w].set(DIFF_D ** -0.5 * LOG2E)
    return (w * scale[None, :]).astype(BF16)


def kernel(x, mem, t5_bias, w_in, w_conv, diff_lambda, diff_subln, w_branch, w_mix_out, w_xq, w_xkv, w_xo, w_ffn_gate, w_ffn_up, w_ffn_down, w_router, w_exp_gate, w_exp_up, w_exp_down, g_pre_mix, g_post_mix, g_pre_xattn, g_mem, g_post_xattn, g_pre_ffn, g_post_ffn):
    B, S, D = x.shape
    assert B == 1
    depth = w_in.shape[0]
    ret_w = RET_HEADS * RET_D
    conv_col0 = 4 * ret_w
    diff_col0 = conv_col0 + 3 * CONV_WIDTH
    gate_col0 = diff_col0 + 3 * DIFF_HEADS * 2 * DIFF_D
    cos2, sin2 = _rope_tables(S)
    near_bias = _near_bias_tiles(t5_bias, ATTN_BLOCK)
    xs = x[0]
    mem2 = mem[0]
    row = lambda g: g[None, :]
    E, _, F = w_exp_gate.shape[1:]
    expert_bf16 = {}
    for l in range(depth):
        moe_next, moe_here = (l + 1) // 2, l // 2
        attn_cast = (w_exp_gate[moe_next].reshape(E * D, F) if l % 2 == 0 and l + 1 < depth
                     else w_exp_up[moe_here].reshape(E * D, F) if l % 2 == 1 else None)
        proj_cast = w_exp_down[moe_here].reshape(E * F, D) if l % 2 == 1 else None
        proj = _in_projection(xs, row(g_pre_mix[l]), _scaled_in_weights(w_in[l]), proj_cast)
        if proj_cast is not None:
            proj, converted = proj
            expert_bf16[moe_here, "down"] = converted.reshape(E, F, D)
        ya = _retention(proj, cos2, sin2)
        lam_init = 0.8 - 0.6 * math.exp(-0.3 * l)
        yc = _diff_attention(proj, near_bias, t5_bias, diff_lambda[l], row(diff_subln[l]), lam_init, attn_cast,
                             col0=diff_col0, t=ATTN_BLOCK)
        if attn_cast is not None:
            yc, converted = yc
            expert_bf16[(moe_next, "gate") if l % 2 == 0 else (moe_here, "up")] = converted.reshape(E, D, F)
        merged = _merge(ya, yc, proj, w_conv[l], w_branch[l].astype(BF16),
                        conv_col0=conv_col0, gate_col0=gate_col0)
        kv = _mem_kv(mem2, row(g_mem[l]), w_xkv[l].astype(BF16))
        xs = _mix_cross_attention(merged, w_mix_out[l].astype(BF16), row(g_post_mix[l]), xs,
                                  row(g_pre_xattn[l]), (w_xq[l] * X_HEAD_DIM ** -0.5).astype(BF16), kv,
                                  w_xo[l].astype(BF16), row(g_post_xattn[l]))
        if l % 2 == 0:
            xs = _ffn(xs, row(g_pre_ffn[l]), w_ffn_gate[l // 2].astype(BF16), w_ffn_up[l // 2].astype(BF16),
                      w_ffn_down[l // 2].astype(BF16), row(g_post_ffn[l]))
        else:
            wr = jnp.pad(w_router[l // 2], ((0, 0), (0, LANES - N_EXPERTS)))
            sel, gate = _router(xs, row(g_pre_ffn[l]), wr)
            tile_e, used, src_token, row_gate, dest = _moe_plan(sel, gate, MOE_TILE)
            y = _moe_group(xs, row(g_pre_ffn[l]), tile_e, used, src_token, row_gate, expert_bf16[moe_here, "gate"],
                           expert_bf16[moe_here, "up"], expert_bf16[moe_here, "down"], tg=MOE_TILE)
            xs = _moe_combine(xs, y, dest, row(g_post_ffn[l]))
    return xs[None]
```

```python
import functools
import math

import numpy as np
import jax
import jax.numpy as jnp
from jax import lax
from jax.experimental import pallas as pl
from jax.experimental.pallas import tpu as pltpu

F32 = jnp.float32
BF16 = jnp.bfloat16

EPS = 1e-6
RET_HEADS = 8
RET_D = 128
RET_CHUNK = 128
ROPE_BASE = 10000.0
CONV_WIDTH = 1024
CONV_K = 3
DIFF_HEADS = 4
DIFF_D = 128
T5_BUCKETS = 32
T5_MAX_DIST = 128
X_HEADS = 4
X_HEAD_DIM = 128
N_EXPERTS = 8
N_BRANCH = 3
BRANCH_W = 1024

V7X_VMEM_BYTES = 64 * 1024 * 1024
VMEM_LIMIT = V7X_VMEM_BYTES - 8 * 1024 * 1024
LANES = 128
NEG_BIG = -1e30
LOG2E = math.log2(math.e)
ATTN_BLOCK = 512
FAR_UNROLL = 4
MOE_TILE = 448
FFN_TF = 1024


def _params(*semantics):
    return pltpu.CompilerParams(dimension_semantics=semantics, vmem_limit_bytes=VMEM_LIMIT)


def _rmsnorm(x, g=None):
    y = x * lax.rsqrt(jnp.mean(x * x, axis=-1, keepdims=True) + EPS)
    return y if g is None else y * g


def _dot(a, b):
    return jnp.dot(a, b, preferred_element_type=F32)


def _dot_nt(a, b):
    return lax.dot_general(a, b, (((1,), (1,)), ((), ())), preferred_element_type=F32)


def _cast_scratch(rows, cols):
    return [pltpu.VMEM((2, rows, cols), F32), pltpu.VMEM((rows, cols), BF16),
            pltpu.SemaphoreType.DMA((2,)), pltpu.SemaphoreType.DMA((1,))]


def _cast_step(step, n_steps, src_hbm, dst_hbm, in_buf, out_buf, in_sem, out_sem):
    rows = out_buf.shape[0]

    def chunk(c):
        return pl.ds(pl.multiple_of(c * rows, rows), rows)

    def in_copy(c, slot):
        return pltpu.make_async_copy(src_hbm.at[chunk(c), :], in_buf.at[slot], in_sem.at[slot])

    def out_copy(c):
        return pltpu.make_async_copy(out_buf, dst_hbm.at[chunk(c), :], out_sem.at[0])

    slot = step % 2

    @pl.when(step == 0)
    def _():
        in_copy(0, 0).start()

    @pl.when(step + 1 < n_steps)
    def _():
        in_copy(step + 1, 1 - slot).start()

    in_copy(step, slot).wait()

    @pl.when(step >= 1)
    def _():
        out_copy(step - 1).wait()

    out_buf[...] = in_buf[slot].astype(BF16)
    out_copy(step).start()

    @pl.when(step == n_steps - 1)
    def _():
        out_copy(step).wait()


def _cast_specs(src, n_steps):
    rows_total, cols = src.shape
    assert rows_total % n_steps == 0
    any_spec = pl.BlockSpec(memory_space=pl.ANY)
    return any_spec, any_spec, jax.ShapeDtypeStruct(src.shape, BF16), _cast_scratch(rows_total // n_steps, cols)


def _in_proj_kernel(x_ref, g_ref, w_ref, *rest, hosts_cast):
    if hosts_cast:
        src_hbm, o_ref, dst_hbm, hn_ref, *cast_scratch = rest
        step = pl.program_id(0) * pl.num_programs(1) + pl.program_id(1)
        _cast_step(step, pl.num_programs(0) * pl.num_programs(1), src_hbm, dst_hbm, *cast_scratch)
    else:
        o_ref, hn_ref = rest

    @pl.when(pl.program_id(1) == 0)
    def _():
        hn_ref[...] = _rmsnorm(x_ref[...], g_ref[...]).astype(BF16)

    o_ref[...] = _dot(hn_ref[...], w_ref[...]).astype(BF16)


def _in_projection(x, g, w, cast_src=None, *, tm=1024):
    S, D = x.shape
    N = w.shape[1]
    tn = 2048 if cast_src is None else 1024
    grid = (S // tm, N // tn)
    in_specs = [pl.BlockSpec((tm, D), lambda i, j: (i, 0)),
                pl.BlockSpec((1, D), lambda i, j: (0, 0)),
                pl.BlockSpec((D, tn), lambda i, j: (0, j))]
    out_specs = [pl.BlockSpec((tm, tn), lambda i, j: (i, j))]
    out_shape = [jax.ShapeDtypeStruct((S, N), BF16)]
    scratch = [pltpu.VMEM((tm, D), BF16)]
    args = [x, g, w]
    if cast_src is not None:
        in_spec, out_spec, shape, cast_scratch = _cast_specs(cast_src, grid[0] * grid[1])
        in_specs.append(in_spec)
        out_specs.append(out_spec)
        out_shape.append(shape)
        scratch += cast_scratch
        args.append(cast_src)
    outs = pl.pallas_call(
        functools.partial(_in_proj_kernel, hosts_cast=cast_src is not None),
        grid=grid,
        in_specs=in_specs,
        out_specs=out_specs,
        out_shape=out_shape,
        scratch_shapes=scratch,
        compiler_params=_params("arbitrary", "arbitrary"),
        name="in_projection",
    )(*args)
    return outs[0] if cast_src is None else tuple(outs)


def _retention_consts():
    C, H = RET_CHUNK, RET_HEADS
    log_g = np.log1p(-(2.0 ** (-5.0 - np.arange(H, dtype=np.float64))))
    idx = np.arange(C, dtype=np.float64)
    dist = idx[:, None] - idx[None, :]
    intra = np.where(dist[None] >= 0, np.exp(np.maximum(dist, 0.0)[None] * log_g[:, None, None]), 0.0)
    kdec = np.exp((C - 1 - idx)[None, :, None] * log_g[:, None, None]) * np.ones((1, 1, RET_D))
    qdec = np.exp((idx + 1.0)[None, :, None] * log_g[:, None, None]) * np.ones((1, 1, RET_D))
    chunk_decay = tuple(float(v) for v in np.exp(C * log_g))
    return (jnp.asarray(intra, F32), jnp.asarray(qdec, F32), jnp.asarray(kdec, F32), chunk_decay)


def _rope(x, cos2, sin2):
    return x * cos2 + pltpu.roll(x, RET_D // 2, 1) * sin2


def _retention_kernel(q_ref, k_ref, v_ref, g_ref, cos_ref, sin_ref, intra_ref, qdec_ref, kdec_ref,
                      o_ref, state_ref, *, chunks, chunk_decay):
    @pl.when(pl.program_id(0) == 0)
    def _():
        state_ref[...] = jnp.zeros_like(state_ref)

    C = RET_CHUNK
    for c in range(chunks):
        rows = slice(c * C, (c + 1) * C)
        cos2 = cos_ref[rows, :]
        sin2 = sin_ref[rows, :]
        for h in range(RET_HEADS):
            cols = slice(h * RET_D, (h + 1) * RET_D)
            q = _rope(q_ref[rows, cols].astype(F32), cos2, sin2)
            k = _rope(k_ref[rows, cols].astype(F32), cos2, sin2)
            v = v_ref[rows, cols]
            state = state_ref[h]
            scores = _dot_nt(q.astype(BF16), k.astype(BF16)) * intra_ref[h]
            o = _dot(scores.astype(BF16), v) + _dot((q * qdec_ref[h]).astype(BF16), state.astype(BF16))
            kv = _dot((k * kdec_ref[h]).T.astype(BF16), v)
            state_ref[h] = state * chunk_decay[h] + kv
            gate = g_ref[rows, cols].astype(F32)
            o_ref[rows, cols] = (gate * jax.nn.sigmoid(gate) * _rmsnorm(o)).astype(BF16)


def _retention(proj, cos2, sin2, *, chunks=2):
    S = proj.shape[0]
    W = RET_HEADS * RET_D
    R = chunks * RET_CHUNK
    intra, qdec, kdec, chunk_decay = _retention_consts()
    col = lambda b: pl.BlockSpec((R, W), lambda i, b=b: (i, b))
    const = pl.BlockSpec((RET_HEADS, RET_CHUNK, RET_D), lambda i: (0, 0, 0))
    rope_spec = pl.BlockSpec((R, RET_D), lambda i: (i, 0))
    return pl.pallas_call(
        functools.partial(_retention_kernel, chunks=chunks, chunk_decay=chunk_decay),
        grid=(S // R,),
        in_specs=[col(0), col(1), col(2), col(3), rope_spec, rope_spec, const, const, const],
        out_specs=pl.BlockSpec((R, W), lambda i: (i, 0)),
        out_shape=jax.ShapeDtypeStruct((S, W), BF16),
        scratch_shapes=[pltpu.VMEM((RET_HEADS, RET_D, RET_D), F32)],
        compiler_params=_params("arbitrary"),
        name="retention",
    )(proj, proj, proj, proj, cos2, sin2, intra, qdec, kdec)


def _t5_bucket(dist):
    max_exact = T5_BUCKETS // 2
    d = jnp.maximum(dist, 0)
    df = jnp.maximum(d, 1).astype(F32)
    large = max_exact + (jnp.log(df / max_exact) / math.log(T5_MAX_DIST / max_exact)
                         * (T5_BUCKETS - max_exact)).astype(jnp.int32)
    large = jnp.minimum(large, T5_BUCKETS - 1)
    return jnp.where(d < max_exact, d, large)


def _near_bias_tiles(t5_bias, t):
    assert t >= T5_MAX_DIST
    ql = jnp.arange(t, dtype=jnp.int32)[:, None]
    kl = jnp.arange(t, dtype=jnp.int32)[None, :]
    tab = t5_bias.T.astype(F32) * LOG2E
    tiles = []
    for off in (0, 1):
        dist = ql + off * t - kl
        bucket = _t5_bucket(dist)[None]
        b = jnp.zeros((tab.shape[0], t, t), F32)
        for n in range(T5_BUCKETS):
            b = jnp.where(bucket == n, tab[:, n, None, None], b)
        tiles.append(jnp.where(dist[None] >= 0, b, NEG_BIG))
    return jnp.stack(tiles, axis=1)


def _diff_attn_kernel(t5_ref, lam_ref, q_ref, k_ref, v_ref, bias_ref, subg_ref, *rest, t, lam_init, hosts_cast):
    h = pl.program_id(0)
    i = pl.program_id(1)
    if hosts_cast:
        src_hbm, o_ref, dst_hbm, mx_ref, l_ref, acc_ref, *cast_scratch = rest
        _cast_step(h * pl.num_programs(1) + i, pl.num_programs(0) * pl.num_programs(1),
                   src_hbm, dst_hbm, *cast_scratch)
    else:
        o_ref, mx_ref, l_ref, acc_ref = rest
    far_bias = t5_ref[h, T5_BUCKETS - 1] * LOG2E
    q = q_ref[...]

    def key_block(j):
        return pl.ds(pl.multiple_of(j * t, t), t)

    def online_block(j, bias, shift):
        k = k_ref[key_block(j), :]
        v = v_ref[key_block(j), :]
        for mp in range(2):
            cols = slice(mp * DIFF_D, (mp + 1) * DIFF_D)
            s = _dot_nt(q[:, cols], k[:, cols])
            if bias is not None:
                s = s + bias
            groups = [s[:, g * LANES:(g + 1) * LANES] for g in range(t // LANES)]
            bm = functools.reduce(jnp.maximum, groups)
            m_old = mx_ref[mp]
            m_new = jnp.maximum(m_old, jnp.max(bm, axis=-1, keepdims=True) + shift)
            alpha = jnp.exp2(m_old - m_new)
            ms = m_new - shift
            ps = [jnp.exp2(g - ms) for g in groups]
            l_ref[mp] = alpha * l_ref[mp] + functools.reduce(lambda a, b: a + b, ps)
            acc_ref[mp] = (jnp.concatenate([alpha, alpha], axis=-1) * acc_ref[mp]
                           + _dot(jnp.concatenate(ps, axis=-1).astype(BF16), v))
            mx_ref[mp] = m_new

    mx_ref[...] = jnp.full_like(mx_ref, NEG_BIG)
    l_ref[...] = jnp.zeros_like(l_ref)
    acc_ref[...] = jnp.zeros_like(acc_ref)

    n = jnp.maximum(i - 1, 0)

    def far_body(jj, carry):
        for u in range(FAR_UNROLL):
            online_block(jj * FAR_UNROLL + u, None, far_bias)
        return carry

    lax.fori_loop(0, n // FAR_UNROLL, far_body, 0)
    for r in range(1, FAR_UNROLL):
        @pl.when(n % FAR_UNROLL == r)
        def _():
            for u in range(r):
                online_block(n - r + u, None, far_bias)

    @pl.when(i >= 1)
    def _():
        online_block(i - 1, bias_ref[1], 0.0)
        online_block(i, bias_ref[0], 0.0)

    @pl.when(i == 0)
    def _():
        online_block(i, bias_ref[0], 0.0)

    lam_p = lam_ref[...]
    lam = (jnp.exp(jnp.sum(lam_p[0:1] * lam_p[1:2], axis=-1, keepdims=True))
           - jnp.exp(jnp.sum(lam_p[2:3] * lam_p[3:4], axis=-1, keepdims=True)) + lam_init)
    l0 = jnp.sum(l_ref[0], axis=-1, keepdims=True)
    l1 = jnp.sum(l_ref[1], axis=-1, keepdims=True)
    o = acc_ref[0] / l0 - lam * (acc_ref[1] / l1)
    o_ref[...] = (_rmsnorm(o, subg_ref[...]) * (1.0 - lam_init)).astype(BF16)


def _diff_attention(proj, bias, t5_bias, lam_params, subln_g, lam_init, cast_src=None, *, col0, t):
    S = proj.shape[0]
    HW = 2 * DIFF_D
    qb = col0 // HW
    kb = qb + DIFF_HEADS
    vb = kb + DIFF_HEADS
    grid = (DIFF_HEADS, S // t)
    in_specs = [pl.BlockSpec(memory_space=pltpu.SMEM),
                pl.BlockSpec((4, DIFF_D), lambda h, i: (0, 0)),
                pl.BlockSpec((t, HW), lambda h, i: (i, qb + h)),
                pl.BlockSpec((S, HW), lambda h, i: (0, kb + h)),
                pl.BlockSpec((S, HW), lambda h, i: (0, vb + h)),
                pl.BlockSpec((None, 2, t, t), lambda h, i: (h, 0, 0, 0), pipeline_mode=pl.Buffered(1)),
                pl.BlockSpec((1, HW), lambda h, i: (0, 0))]
    out_specs = [pl.BlockSpec((t, HW), lambda h, i: (i, h))]
    out_shape = [jax.ShapeDtypeStruct((S, DIFF_HEADS * HW), BF16)]
    scratch = [pltpu.VMEM((2, t, LANES), F32), pltpu.VMEM((2, t, LANES), F32), pltpu.VMEM((2, t, HW), F32)]
    args = [t5_bias.T.astype(F32), lam_params, proj, proj, proj, bias, subln_g]
    if cast_src is not None:
        in_spec, out_spec, shape, cast_scratch = _cast_specs(cast_src, grid[0] * grid[1])
        in_specs.append(in_spec)
        out_specs.append(out_spec)
        out_shape.append(shape)
        scratch += cast_scratch
        args.append(cast_src)
    outs = pl.pallas_call(
        functools.partial(_diff_attn_kernel, t=t, lam_init=lam_init, hosts_cast=cast_src is not None),
        grid=grid,
        in_specs=in_specs,
        out_specs=out_specs,
        out_shape=out_shape,
        scratch_shapes=scratch,
        compiler_params=_params("arbitrary", "arbitrary"),
        name="diff_attention",
    )(*args)
    return outs[0] if cast_src is None else tuple(outs)


def _merge_kernel(ya_ref, cb_ref, cc_ref, cu_ref, hc_ref, hu_ref, wconv_ref, yc_ref,
                  g0_ref, g1_ref, g2_ref, wb_ref, o_ref, *, halo):
    i = pl.program_id(0)
    v = cc_ref[...].astype(F32) * cu_ref[...].astype(F32)
    hv = jnp.where(i > 0, hc_ref[...].astype(F32) * hu_ref[...].astype(F32), 0.0)
    prev1 = hv[halo - 1:halo, :]
    prev2 = hv[halo - 2:halo - 1, :]
    row = lax.broadcasted_iota(jnp.int32, v.shape, 0)
    v1 = jnp.where(row == 0, prev1, pltpu.roll(v, 1, 0))
    v2 = jnp.where(row == 0, prev2, jnp.where(row == 1, prev1, pltpu.roll(v, 2, 0)))
    w = wconv_ref[...]
    yb = cb_ref[...].astype(F32) * (w[0:1] * v2 + w[1:2] * v1 + w[2:3] * v)
    merged = jax.nn.sigmoid(g0_ref[...].astype(F32)) * _dot(ya_ref[...], wb_ref[0])
    merged += jax.nn.sigmoid(g1_ref[...].astype(F32)) * _dot(yb.astype(BF16), wb_ref[1])
    merged += jax.nn.sigmoid(g2_ref[...].astype(F32)) * _dot(yc_ref[...], wb_ref[2])
    o_ref[...] = merged.astype(BF16)


def _merge(ya, yc, proj, w_conv, w_branch, *, conv_col0, gate_col0, tm=256, halo=16):
    S = proj.shape[0]
    W = BRANCH_W
    D = w_branch.shape[-1]
    cblk = conv_col0 // W
    gblk = gate_col0 // D
    hb = tm // halo
    row = lambda b: pl.BlockSpec((tm, W), lambda i, b=b: (i, b))
    halo_spec = lambda b: pl.BlockSpec((halo, W), lambda i, b=b: (jnp.maximum(i * hb - 1, 0), b))
    gate = lambda b: pl.BlockSpec((tm, D), lambda i, b=b: (i, b))
    return pl.pallas_call(
        functools.partial(_merge_kernel, halo=halo),
        grid=(S // tm,),
        in_specs=[pl.BlockSpec((tm, W), lambda i: (i, 0)),
                  row(cblk), row(cblk + 1), row(cblk + 2), halo_spec(cblk + 1), halo_spec(cblk + 2),
                  pl.BlockSpec((CONV_K, W), lambda i: (0, 0)),
                  pl.BlockSpec((tm, W), lambda i: (i, 0)),
                  gate(gblk), gate(gblk + 1), gate(gblk + 2),
                  pl.BlockSpec((N_BRANCH, W, D), lambda i: (0, 0, 0))],
        out_specs=pl.BlockSpec((tm, D), lambda i: (i, 0)),
        out_shape=jax.ShapeDtypeStruct((S, D), BF16),
        compiler_params=_params("parallel"),
        name="conv_gated_merge",
    )(ya, proj, proj, proj, proj, proj, w_conv, yc, proj, proj, proj, w_branch)


def _mem_kv_kernel(mem_ref, g_ref, w_ref, o_ref):
    o_ref[...] = _dot(_rmsnorm(mem_ref[...], g_ref[...]).astype(BF16), w_ref[...]).astype(BF16)


def _mem_kv(mem, g, wkv):
    M, D = mem.shape
    N = wkv.shape[1]
    return pl.pallas_call(
        _mem_kv_kernel,
        out_shape=jax.ShapeDtypeStruct((M, N), BF16),
        compiler_params=pltpu.CompilerParams(vmem_limit_bytes=VMEM_LIMIT),
        name="memory_kv",
    )(mem, g, wkv)


def _mix_xattn_kernel(a_ref, wmix_ref, gmix_ref, x_ref, gpre_ref, wq_ref, kv_ref, wo_ref, gpost_ref, o_ref):
    x = x_ref[...] + _rmsnorm(_dot(a_ref[...], wmix_ref[...]), gmix_ref[...])
    q = _dot(_rmsnorm(x, gpre_ref[...]).astype(BF16), wq_ref[...]).astype(BF16)
    width = X_HEADS * X_HEAD_DIM
    outs = []
    for h in range(X_HEADS):
        cols = slice(h * X_HEAD_DIM, (h + 1) * X_HEAD_DIM)
        k = kv_ref[:, cols]
        v = kv_ref[:, width + h * X_HEAD_DIM: width + (h + 1) * X_HEAD_DIM]
        s = _dot_nt(q[:, cols], k)
        p = jnp.exp(s - jnp.max(s, axis=-1, keepdims=True))
        p = p / jnp.sum(p, axis=-1, keepdims=True)
        outs.append(_dot(p.astype(BF16), v).astype(BF16))
    o = jnp.concatenate(outs, axis=-1)
    o_ref[...] = x + _rmsnorm(_dot(o, wo_ref[...]), gpost_ref[...])


def _mix_cross_attention(a, w_mix, g_mix, x, g_pre, wq, kv, wo, g_post, *, tm=512):
    S, D = x.shape
    K = a.shape[1]
    full = lambda arr: pl.BlockSpec(arr.shape, lambda i: (0,) * arr.ndim, pipeline_mode=pl.Buffered(1))
    return pl.pallas_call(
        _mix_xattn_kernel,
        grid=(S // tm,),
        in_specs=[pl.BlockSpec((tm, K), lambda i: (i, 0)), full(w_mix), full(g_mix),
                  pl.BlockSpec((tm, D), lambda i: (i, 0)), full(g_pre), full(wq), full(kv), full(wo), full(g_post)],
        out_specs=pl.BlockSpec((tm, D), lambda i: (i, 0)),
        out_shape=jax.ShapeDtypeStruct((S, D), F32),
        compiler_params=_params("parallel"),
        name="mix_out_cross_attention",
    )(a, w_mix, g_mix, x, g_pre, wq, kv, wo, g_post)


def _swiglu_partial(hn, wg, wu, wd):
    a = _dot(hn, wg)
    u = _dot(hn, wu)
    return _dot((a * jax.nn.sigmoid(a) * u).astype(BF16), wd)


def _ffn_kernel(x_ref, gpre_ref, wg_ref, wu_ref, wd_ref, gpost_ref, o_ref, hn_ref, acc_ref):
    f = pl.program_id(1)

    @pl.when(f == 0)
    def _():
        hn_ref[...] = _rmsnorm(x_ref[...], gpre_ref[...]).astype(BF16)
        acc_ref[...] = jnp.zeros_like(acc_ref)

    acc_ref[...] += _swiglu_partial(hn_ref[...], wg_ref[...], wu_ref[...], wd_ref[...])

    @pl.when(f == pl.num_programs(1) - 1)
    def _():
        o_ref[...] = x_ref[...] + _rmsnorm(acc_ref[...], gpost_ref[...])


def _ffn(x, g_pre, wg, wu, wd, g_post, *, tm=512, tf=FFN_TF):
    S, D = x.shape
    F = wg.shape[1]
    return pl.pallas_call(
        _ffn_kernel,
        grid=(S // tm, F // tf),
        in_specs=[pl.BlockSpec((tm, D), lambda i, f: (i, 0)),
                  pl.BlockSpec((1, D), lambda i, f: (0, 0)),
                  pl.BlockSpec((D, tf), lambda i, f: (0, f)),
                  pl.BlockSpec((D, tf), lambda i, f: (0, f)),
                  pl.BlockSpec((tf, D), lambda i, f: (f, 0)),
                  pl.BlockSpec((1, D), lambda i, f: (0, 0))],
        out_specs=pl.BlockSpec((tm, D), lambda i, f: (i, 0)),
        out_shape=jax.ShapeDtypeStruct((S, D), F32),
        scratch_shapes=[pltpu.VMEM((tm, D), BF16), pltpu.VMEM((tm, D), F32)],
        compiler_params=_params("parallel", "arbitrary"),
        name="swiglu_ffn",
    )(x, g_pre, wg, wu, wd, g_post)


def _router_kernel(x_ref, g_ref, wr_ref, sel_ref, gate_ref):
    hn = _rmsnorm(x_ref[...], g_ref[...])
    logits = jnp.dot(hn, wr_ref[...], preferred_element_type=F32, precision=lax.Precision.HIGHEST)
    lane = lax.broadcasted_iota(jnp.int32, logits.shape, 1)
    neg_inf = -jnp.inf
    lg = jnp.where(lane < N_EXPERTS, logits, neg_inf)
    m1 = jnp.max(lg, axis=-1, keepdims=True)
    i1 = jnp.min(jnp.where(lg == m1, lane, LANES), axis=-1, keepdims=True)
    lg2 = jnp.where(lane == i1, neg_inf, lg)
    m2 = jnp.max(lg2, axis=-1, keepdims=True)
    i2 = jnp.min(jnp.where(lg2 == m2, lane, LANES), axis=-1, keepdims=True)
    e = jnp.exp(m2 - m1)
    g1 = 1.0 / (1.0 + e)
    sel_ref[...] = jnp.where(lane == 0, i1, jnp.where(lane == 1, i2, 0))
    gate_ref[...] = jnp.where(lane == 0, g1, jnp.where(lane == 1, e * g1, 0.0))


def _router(x, g_pre, w_router_padded, *, tm=512):
    S, D = x.shape
    return pl.pallas_call(
        _router_kernel,
        grid=(S // tm,),
        in_specs=[pl.BlockSpec((tm, D), lambda i: (i, 0)),
                  pl.BlockSpec((1, D), lambda i: (0, 0)),
                  pl.BlockSpec((D, LANES), lambda i: (0, 0))],
        out_specs=[pl.BlockSpec((tm, LANES), lambda i: (i, 0)),
                   pl.BlockSpec((tm, LANES), lambda i: (i, 0))],
        out_shape=[jax.ShapeDtypeStruct((S, LANES), jnp.int32),
                   jax.ShapeDtypeStruct((S, LANES), F32)],
        compiler_params=_params("parallel"),
        name="moe_router",
    )(x, g_pre, w_router_padded)


def _moe_plan(sel, gate, tg):
    S = sel.shape[0]
    P, E = 2 * S, N_EXPERTS
    i32 = jnp.int32
    e_flat = sel[:, :2].reshape(P)
    g_flat = gate[:, :2].reshape(P)
    onehot = (e_flat[:, None] == jnp.arange(E, dtype=i32)[None, :]).astype(i32)
    counts = jnp.sum(onehot, axis=0)
    rank = jnp.sum((jnp.cumsum(onehot, axis=0) - 1) * onehot, axis=1)
    ustart = jnp.cumsum(counts) - counts
    tiles = (counts + tg - 1) // tg
    tile_end = jnp.cumsum(tiles)
    tstart = tile_end - tiles
    n_tiles = P // tg + E
    n = jnp.arange(n_tiles, dtype=i32)
    tile_e = jnp.minimum(jnp.sum((n[:, None] >= tile_end[None, :]).astype(i32), axis=1), E - 1)
    used = n < tile_end[-1]
    local = ((n - tstart[tile_e]) * tg)[:, None] + jnp.arange(tg, dtype=i32)[None, :]
    valid = used[:, None] & (local < counts[tile_e][:, None])
    order = jnp.argsort(e_flat, stable=True).astype(i32)
    pair = order[jnp.clip(ustart[tile_e][:, None] + local, 0, P - 1)]
    src_token = jnp.concatenate([jnp.where(valid, pair // 2, 0).reshape(-1), jnp.zeros((tg,), i32)])
    row_gate = jnp.where(valid, g_flat[pair], 0.0).reshape(-1, 1)
    dest = tstart[e_flat] * tg + rank
    return tile_e, used.astype(i32), src_token, row_gate, dest


def _row_gather(idx_ref, base, src_hbm, dst_ref, sem, first, n_rows, *, wait):
    for u in range(n_rows):
        r = first + u
        cp = pltpu.make_async_copy(src_hbm.at[pl.ds(idx_ref[base + r], 1), :], dst_ref.at[pl.ds(r, 1), :], sem)
        if wait:
            cp.wait()
        else:
            cp.start()


def _moe_group_kernel(te_ref, used_ref, src_ref, x_hbm, gpre_ref, gate_ref, wg_ref, wu_ref, wd_ref, y_ref,
                      xg_ref, hb_ref, acc_ref, sem, *, tg, nf):
    n = pl.program_id(0)
    f = pl.program_id(1)
    slot = n % 2
    share = tg // nf
    requested = (n == 0) | (used_ref[jnp.maximum(n - 1, 0)] > 0)

    @pl.when(f == 0)
    def _():
        @pl.when(n == 0)
        def _():
            _row_gather(src_ref, 0, x_hbm, xg_ref.at[0], sem.at[0], 0, tg, wait=False)

        @pl.when(requested)
        def _():
            _row_gather(src_ref, n * tg, x_hbm, xg_ref.at[slot], sem.at[slot], 0, tg, wait=True)
            hb_ref[...] = _rmsnorm(xg_ref[slot], gpre_ref[...]).astype(BF16)

        acc_ref[...] = jnp.zeros_like(acc_ref)

    @pl.when(used_ref[n] > 0)
    def _():
        _row_gather(src_ref, (n + 1) * tg, x_hbm, xg_ref.at[1 - slot], sem.at[1 - slot], f * share, share,
                    wait=False)
        acc_ref[...] += _swiglu_partial(hb_ref[...], wg_ref[...], wu_ref[...], wd_ref[...])

    @pl.when(f == nf - 1)
    def _():
        y_ref[...] = acc_ref[...] * gate_ref[...]

        @pl.when((n == pl.num_programs(0) - 1) & (used_ref[n] > 0))
        def _():
            _row_gather(src_ref, (n + 1) * tg, x_hbm, xg_ref.at[1 - slot], sem.at[1 - slot], 0, tg, wait=True)


def _moe_group(x, g_pre, tile_e, used, src_token, row_gate, wg, wu, wd, *, tg, tf=FFN_TF):
    S, D = x.shape
    E, _, F = wg.shape
    n_tiles = tile_e.shape[0]
    nf = F // tf
    assert tg % nf == 0
    fblk = lambda n, f, used: jnp.where(used[n] > 0, f, nf - 1)
    grid_spec = pltpu.PrefetchScalarGridSpec(
        num_scalar_prefetch=3,
        grid=(n_tiles, nf),
        in_specs=[pl.BlockSpec(memory_space=pl.ANY),
                  pl.BlockSpec((1, D), lambda n, f, te, used, src: (0, 0)),
                  pl.BlockSpec((tg, 1), lambda n, f, te, used, src: (n, 0)),
                  pl.BlockSpec((None, D, tf), lambda n, f, te, used, src: (te[n], 0, fblk(n, f, used))),
                  pl.BlockSpec((None, D, tf), lambda n, f, te, used, src: (te[n], 0, fblk(n, f, used))),
                  pl.BlockSpec((None, tf, D), lambda n, f, te, used, src: (te[n], fblk(n, f, used), 0))],
        out_specs=pl.BlockSpec((tg, D), lambda n, f, te, used, src: (n, 0)),
        scratch_shapes=[pltpu.VMEM((2, tg, D), F32), pltpu.VMEM((tg, D), BF16), pltpu.VMEM((tg, D), F32),
                        pltpu.SemaphoreType.DMA((2,))],
    )
    return pl.pallas_call(
        functools.partial(_moe_group_kernel, tg=tg, nf=nf),
        grid_spec=grid_spec,
        out_shape=jax.ShapeDtypeStruct((n_tiles * tg, D), F32),
        compiler_params=_params("arbitrary", "arbitrary"),
        name="moe_grouped_swiglu",
    )(tile_e, used, src_token, x, g_pre, row_gate, wg, wu, wd)


def _moe_combine_kernel(dest_ref, x_ref, y_hbm, g_ref, o_ref, yb_ref, sem, *, tm):
    i = pl.program_id(0)
    slot = i % 2

    def gather(tile, s, wait):
        base = tile * (2 * tm)
        for r in range(tm):
            for k in range(2):
                row = dest_ref[base + 2 * r + k]
                cp = pltpu.make_async_copy(y_hbm.at[pl.ds(row, 1), :], yb_ref.at[s, k, pl.ds(r, 1), :], sem.at[s])
                if wait:
                    cp.wait()
                else:
                    cp.start()

    @pl.when(i == 0)
    def _():
        gather(0, 0, False)

    @pl.when(i + 1 < pl.num_programs(0))
    def _():
        gather(i + 1, 1 - slot, False)

    gather(i, slot, True)
    o_ref[...] = x_ref[...] + _rmsnorm(yb_ref[slot, 0] + yb_ref[slot, 1], g_ref[...])


def _moe_combine(x, y, dest, g_post, *, tm=256):
    S, D = x.shape
    grid_spec = pltpu.PrefetchScalarGridSpec(
        num_scalar_prefetch=1,
        grid=(S // tm,),
        in_specs=[pl.BlockSpec((tm, D), lambda i, dest: (i, 0)),
                  pl.BlockSpec(memory_space=pl.ANY),
                  pl.BlockSpec((1, D), lambda i, dest: (0, 0))],
        out_specs=pl.BlockSpec((tm, D), lambda i, dest: (i, 0)),
        scratch_shapes=[pltpu.VMEM((2, 2, tm, D), F32), pltpu.SemaphoreType.DMA((2,))],
    )
    return pl.pallas_call(
        functools.partial(_moe_combine_kernel, tm=tm),
        grid_spec=grid_spec,
        out_shape=jax.ShapeDtypeStruct((S, D), F32),
        compiler_params=_params("arbitrary"),
        name="moe_combine",
    )(dest, x, y, g_post)


def _rope_tables(S):
    half = RET_D // 2
    inv = ROPE_BASE ** (-jnp.arange(half, dtype=F32) / half)
    ang = jnp.arange(S, dtype=jnp.int32)[:, None].astype(F32) * inv[None, :]
    cos, sin = jnp.cos(ang), jnp.sin(ang)
    return jnp.concatenate([cos, cos], axis=-1), jnp.concatenate([-sin, sin], axis=-1)


def _scaled_in_weights(w):
    ret_w = RET_HEADS * RET_D
    diff_col0 = 4 * ret_w + 3 * CONV_WIDTH
    diff_w = DIFF_HEADS * 2 * DIFF_D
    scale = jnp.ones((w.shape[1],), F32)
    scale = scale.at[:ret_w].set(RET_D ** -0.5)
    scale = scale.at[diff_col0:diff_col0 + diff_w].set(DIFF_D ** -0.5 * LOG2E)
    return (w * scale[None, :]).astype(BF16)


def kernel(x, mem, t5_bias, w_in, w_conv, diff_lambda, diff_subln, w_branch, w_mix_out, w_xq, w_xkv, w_xo, w_ffn_gate, w_ffn_up, w_ffn_down, w_router, w_exp_gate, w_exp_up, w_exp_down, g_pre_mix, g_post_mix, g_pre_xattn, g_mem, g_post_xattn, g_pre_ffn, g_post_ffn):
    B, S, D = x.shape
    assert B == 1
    depth = w_in.shape[0]
    ret_w = RET_HEADS * RET_D
    conv_col0 = 4 * ret_w
    diff_col0 = conv_col0 + 3 * CONV_WIDTH
    gate_col0 = diff_col0 + 3 * DIFF_HEADS * 2 * DIFF_D
    cos2, sin2 = _rope_tables(S)
    near_bias = _near_bias_tiles(t5_bias, ATTN_BLOCK)
    xs = x[0]
    mem2 = mem[0]
    row = lambda g: g[None, :]
    E, _, F = w_exp_gate.shape[1:]
    expert_bf16 = {}
    for l in range(depth):
        moe_next, moe_here = (l + 1) // 2, l // 2
        attn_cast = (w_exp_gate[moe_next].reshape(E * D, F) if l % 2 == 0 and l + 1 < depth
                     else w_exp_up[moe_here].reshape(E * D, F) if l % 2 == 1 else None)
        proj_cast = w_exp_down[moe_here].reshape(E * F, D) if l % 2 == 1 else None
        proj = _in_projection(xs, row(g_pre_mix[l]), _scaled_in_weights(w_in[l]), proj_cast)
        if proj_cast is not None:
            proj, converted = proj
            expert_bf16[moe_here, "down"] = converted.reshape(E, F, D)
        ya = _retention(proj, cos2, sin2)
        lam_init = 0.8 - 0.6 * math.exp(-0.3 * l)
        yc = _diff_attention(proj, near_bias, t5_bias, diff_lambda[l], row(diff_subln[l]), lam_init, attn_cast,
                             col0=diff_col0, t=ATTN_BLOCK)
        if attn_cast is not None:
            yc, converted = yc
            expert_bf16[(moe_next, "gate") if l % 2 == 0 else (moe_here, "up")] = converted.reshape(E, D, F)
        merged = _merge(ya, yc, proj, w_conv[l], w_branch[l].astype(BF16),
                        conv_col0=conv_col0, gate_col0=gate_col0)
        kv = _mem_kv(mem2, row(g_mem[l]), w_xkv[l].astype(BF16))
        xs = _mix_cross_attention(merged, w_mix_out[l].astype(BF16), row(g_post_mix[l]), xs,
                                  row(g_pre_xattn[l]), (w_xq[l] * X_HEAD_DIM ** -0.5).astype(BF16), kv,
                                  w_xo[l].astype(BF16), row(g_post_xattn[l]))
        if l % 2 == 0:
            xs = _ffn(xs, row(g_pre_ffn[l]), w_ffn_gate[l // 2].astype(BF16), w_ffn_up[l // 2].astype(BF16),
                      w_ffn_down[l // 2].astype(BF16), row(g_post_ffn[l]))
        else:
            wr = jnp.pad(w_router[l // 2], ((0, 0), (0, LANES - N_EXPERTS)))
            sel, gate = _router(xs, row(g_pre_ffn[l]), wr)
            tile_e, used, src_token, row_gate, dest = _moe_plan(sel, gate, MOE_TILE)
            y = _moe_group(xs, row(g_pre_ffn[l]), tile_e, used, src_token, row_gate, expert_bf16[moe_here, "gate"],
                           expert_bf16[moe_here, "up"], expert_bf16[moe_here, "down"], tg=MOE_TILE)
            xs = _moe_combine(xs, y, dest, row(g_post_ffn[l]))
    return xs[None]
```

```python
import functools
import math

import numpy as np
import jax
import jax.numpy as jnp
from jax import lax
from jax.experimental import pallas as pl
from jax.experimental.pallas import tpu as pltpu

F32 = jnp.float32
BF16 = jnp.bfloat16

EPS = 1e-6
RET_HEADS = 8
RET_D = 128
RET_CHUNK = 128
ROPE_BASE = 10000.0
CONV_WIDTH = 1024
CONV_K = 3
DIFF_HEADS = 4
DIFF_D = 128
T5_BUCKETS = 32
T5_MAX_DIST = 128
X_HEADS = 4
X_HEAD_DIM = 128
N_EXPERTS = 8
N_BRANCH = 3
BRANCH_W = 1024

V7X_VMEM_BYTES = 64 * 1024 * 1024
VMEM_LIMIT = V7X_VMEM_BYTES - 8 * 1024 * 1024
LANES = 128
NEG_BIG = -1e30
LOG2E = math.log2(math.e)
ATTN_BLOCK = 512
FAR_UNROLL = 4
MOE_TILE = 448
FFN_TF = 1024


def _params(*semantics):
    return pltpu.CompilerParams(dimension_semantics=semantics, vmem_limit_bytes=VMEM_LIMIT)


def _rmsnorm(x, g=None):
    y = x * lax.rsqrt(jnp.mean(x * x, axis=-1, keepdims=True) + EPS)
    return y if g is None else y * g


def _dot(a, b):
    return jnp.dot(a, b, preferred_element_type=F32)


def _dot_nt(a, b):
    return lax.dot_general(a, b, (((1,), (1,)), ((), ())), preferred_element_type=F32)


def _cast_scratch(rows, cols):
    return [pltpu.VMEM((2, rows, cols), F32), pltpu.VMEM((rows, cols), BF16),
            pltpu.SemaphoreType.DMA((2,)), pltpu.SemaphoreType.DMA((1,))]


def _cast_step(step, n_steps, src_hbm, dst_hbm, in_buf, out_buf, in_sem, out_sem):
    rows = out_buf.shape[0]

    def chunk(c):
        return pl.ds(pl.multiple_of(c * rows, rows), rows)

    def in_copy(c, slot):
        return pltpu.make_async_copy(src_hbm.at[chunk(c), :], in_buf.at[slot], in_sem.at[slot])

    def out_copy(c):
        return pltpu.make_async_copy(out_buf, dst_hbm.at[chunk(c), :], out_sem.at[0])

    slot = step % 2

    @pl.when(step == 0)
    def _():
        in_copy(0, 0).start()

    @pl.when(step + 1 < n_steps)
    def _():
        in_copy(step + 1, 1 - slot).start()

    in_copy(step, slot).wait()

    @pl.when(step >= 1)
    def _():
        out_copy(step - 1).wait()

    out_buf[...] = in_buf[slot].astype(BF16)
    out_copy(step).start()

    @pl.when(step == n_steps - 1)
    def _():
        out_copy(step).wait()


def _cast_specs(src, n_steps):
    rows_total, cols = src.shape
    assert rows_total % n_steps == 0
    any_spec = pl.BlockSpec(memory_space=pl.ANY)
    return any_spec, any_spec, jax.ShapeDtypeStruct(src.shape, BF16), _cast_scratch(rows_total // n_steps, cols)


def _in_proj_kernel(x_ref, g_ref, w_ref, *rest, hosts_cast):
    if hosts_cast:
        src_hbm, o_ref, dst_hbm, hn_ref, *cast_scratch = rest
        step = pl.program_id(0) * pl.num_programs(1) + pl.program_id(1)
        _cast_step(step, pl.num_programs(0) * pl.num_programs(1), src_hbm, dst_hbm, *cast_scratch)
    else:
        o_ref, hn_ref = rest

    @pl.when(pl.program_id(1) == 0)
    def _():
        hn_ref[...] = _rmsnorm(x_ref[...], g_ref[...]).astype(BF16)

    o_ref[...] = _dot(hn_ref[...], w_ref[...]).astype(BF16)


def _in_projection(x, g, w, cast_src=None, *, tm=1024):
    S, D = x.shape
    N = w.shape[1]
    tn = 2048 if cast_src is None else 1024
    grid = (S // tm, N // tn)
    in_specs = [pl.BlockSpec((tm, D), lambda i, j: (i, 0)),
                pl.BlockSpec((1, D), lambda i, j: (0, 0)),
                pl.BlockSpec((D, tn), lambda i, j: (0, j))]
    out_specs = [pl.BlockSpec((tm, tn), lambda i, j: (i, j))]
    out_shape = [jax.ShapeDtypeStruct((S, N), BF16)]
    scratch = [pltpu.VMEM((tm, D), BF16)]
    args = [x, g, w]
    if cast_src is not None:
        in_spec, out_spec, shape, cast_scratch = _cast_specs(cast_src, grid[0] * grid[1])
        in_specs.append(in_spec)
        out_specs.append(out_spec)
        out_shape.append(shape)
        scratch += cast_scratch
        args.append(cast_src)
    outs = pl.pallas_call(
        functools.partial(_in_proj_kernel, hosts_cast=cast_src is not None),
        grid=grid,
        in_specs=in_specs,
        out_specs=out_specs,
        out_shape=out_shape,
        scratch_shapes=scratch,
        compiler_params=_params("arbitrary", "arbitrary"),
        name="in_projection",
    )(*args)
    return outs[0] if cast_src is None else tuple(outs)


def _retention_consts():
    C, H = RET_CHUNK, RET_HEADS
    log_g = np.log1p(-(2.0 ** (-5.0 - np.arange(H, dtype=np.float64))))
    idx = np.arange(C, dtype=np.float64)
    dist = idx[:, None] - idx[None, :]
    intra = np.where(dist[None] >= 0, np.exp(np.maximum(dist, 0.0)[None] * log_g[:, None, None]), 0.0)
    kdec = np.exp((C - 1 - idx)[None, :, None] * log_g[:, None, None]) * np.ones((1, 1, RET_D))
    qdec = np.exp((idx + 1.0)[None, :, None] * log_g[:, None, None]) * np.ones((1, 1, RET_D))
    chunk_decay = tuple(float(v) for v in np.exp(C * log_g))
    return (jnp.asarray(intra, F32), jnp.asarray(qdec, F32), jnp.asarray(kdec, F32), chunk_decay)


def _rope(x, cos2, sin2):
    return x * cos2 + pltpu.roll(x, RET_D // 2, 1) * sin2


def _retention_kernel(q_ref, k_ref, v_ref, g_ref, cos_ref, sin_ref, intra_ref, qdec_ref, kdec_ref,
                      o_ref, state_ref, *, chunks, chunk_decay):
    @pl.when(pl.program_id(0) == 0)
    def _():
        state_ref[...] = jnp.zeros_like(state_ref)

    C = RET_CHUNK
    for c in range(chunks):
        rows = slice(c * C, (c + 1) * C)
        cos2 = cos_ref[rows, :]
        sin2 = sin_ref[rows, :]
        for h in range(RET_HEADS):
            cols = slice(h * RET_D, (h + 1) * RET_D)
            q = _rope(q_ref[rows, cols].astype(F32), cos2, sin2)
            k = _rope(k_ref[rows, cols].astype(F32), cos2, sin2)
            v = v_ref[rows, cols]
            state = state_ref[h]
            scores = _dot_nt(q.astype(BF16), k.astype(BF16)) * intra_ref[h]
            o = _dot(scores.astype(BF16), v) + _dot((q * qdec_ref[h]).astype(BF16), state.astype(BF16))
            kv = _dot((k * kdec_ref[h]).T.astype(BF16), v)
            state_ref[h] = state * chunk_decay[h] + kv
            gate = g_ref[rows, cols].astype(F32)
            o_ref[rows, cols] = (gate * jax.nn.sigmoid(gate) * _rmsnorm(o)).astype(BF16)


def _retention(proj, cos2, sin2, *, chunks=2):
    S = proj.shape[0]
    W = RET_HEADS * RET_D
    R = chunks * RET_CHUNK
    intra, qdec, kdec, chunk_decay = _retention_consts()
    col = lambda b: pl.BlockSpec((R, W), lambda i, b=b: (i, b))
    const = pl.BlockSpec((RET_HEADS, RET_CHUNK, RET_D), lambda i: (0, 0, 0))
    rope_spec = pl.BlockSpec((R, RET_D), lambda i: (i, 0))
    return pl.pallas_call(
        functools.partial(_retention_kernel, chunks=chunks, chunk_decay=chunk_decay),
        grid=(S // R,),
        in_specs=[col(0), col(1), col(2), col(3), rope_spec, rope_spec, const, const, const],
        out_specs=pl.BlockSpec((R, W), lambda i: (i, 0)),
        out_shape=jax.ShapeDtypeStruct((S, W), BF16),
        scratch_shapes=[pltpu.VMEM((RET_HEADS, RET_D, RET_D), F32)],
        compiler_params=_params("arbitrary"),
        name="retention",
    )(proj, proj, proj, proj, cos2, sin2, intra, qdec, kdec)


def _t5_bucket(dist):
    max_exact = T5_BUCKETS // 2
    d = jnp.maximum(dist, 0)
    df = jnp.maximum(d, 1).astype(F32)
    large = max_exact + (jnp.log(df / max_exact) / math.log(T5_MAX_DIST / max_exact)
                         * (T5_BUCKETS - max_exact)).astype(jnp.int32)
    large = jnp.minimum(large, T5_BUCKETS - 1)
    return jnp.where(d < max_exact, d, large)


def _near_bias_tiles(t5_bias, t):
    assert t >= T5_MAX_DIST
    ql = jnp.arange(t, dtype=jnp.int32)[:, None]
    kl = jnp.arange(t, dtype=jnp.int32)[None, :]
    tab = t5_bias.T.astype(F32) * LOG2E
    tiles = []
    for off in (0, 1):
        dist = ql + off * t - kl
        bucket = _t5_bucket(dist)[None]
        b = jnp.zeros((tab.shape[0], t, t), F32)
        for n in range(T5_BUCKETS):
            b = jnp.where(bucket == n, tab[:, n, None, None], b)
        tiles.append(jnp.where(dist[None] >= 0, b, NEG_BIG))
    return jnp.stack(tiles, axis=1)


def _diff_attn_kernel(t5_ref, lam_ref, q_ref, k_ref, v_ref, bias_ref, subg_ref, *rest, t, lam_init, hosts_cast):
    h = pl.program_id(0)
    i = pl.program_id(1)
    if hosts_cast:
        src_hbm, o_ref, dst_hbm, mx_ref, l_ref, acc_ref, *cast_scratch = rest
        _cast_step(h * pl.num_programs(1) + i, pl.num_programs(0) * pl.num_programs(1),
                   src_hbm, dst_hbm, *cast_scratch)
    else:
        o_ref, mx_ref, l_ref, acc_ref = rest
    far_bias = t5_ref[h, T5_BUCKETS - 1] * LOG2E
    q = q_ref[...]

    def key_block(j):
        return pl.ds(pl.multiple_of(j * t, t), t)

    def online_block(j, bias, shift):
        k = k_ref[key_block(j), :]
        v = v_ref[key_block(j), :]
        for mp in range(2):
            cols = slice(mp * DIFF_D, (mp + 1) * DIFF_D)
            s = _dot_nt(q[:, cols], k[:, cols])
            if bias is not None:
                s = s + bias
            groups = [s[:, g * LANES:(g + 1) * LANES] for g in range(t // LANES)]
            bm = functools.reduce(jnp.maximum, groups)
            m_old = mx_ref[mp]
            m_new = jnp.maximum(m_old, jnp.max(bm, axis=-1, keepdims=True) + shift)
            alpha = jnp.exp2(m_old - m_new)
            ms = m_new - shift
            ps = [jnp.exp2(g - ms) for g in groups]
            l_ref[mp] = alpha * l_ref[mp] + functools.reduce(lambda a, b: a + b, ps)
            acc_ref[mp] = (jnp.concatenate([alpha, alpha], axis=-1) * acc_ref[mp]
                           + _dot(jnp.concatenate(ps, axis=-1).astype(BF16), v))
            mx_ref[mp] = m_new

    mx_ref[...] = jnp.full_like(mx_ref, NEG_BIG)
    l_ref[...] = jnp.zeros_like(l_ref)
    acc_ref[...] = jnp.zeros_like(acc_ref)

    n = jnp.maximum(i - 1, 0)

    def far_body(jj, carry):
        for u in range(FAR_UNROLL):
            online_block(jj * FAR_UNROLL + u, None, far_bias)
        return carry

    lax.fori_loop(0, n // FAR_UNROLL, far_body, 0)
    for r in range(1, FAR_UNROLL):
        @pl.when(n % FAR_UNROLL == r)
        def _():
            for u in range(r):
                online_block(n - r + u, None, far_bias)

    @pl.when(i >= 1)
    def _():
        online_block(i - 1, bias_ref[1], 0.0)
        online_block(i, bias_ref[0], 0.0)

    @pl.when(i == 0)
    def _():
        online_block(i, bias_ref[0], 0.0)

    lam_p = lam_ref[...]
    lam = (jnp.exp(jnp.sum(lam_p[0:1] * lam_p[1:2], axis=-1, keepdims=True))
           - jnp.exp(jnp.sum(lam_p[2:3] * lam_p[3:4], axis=-1, keepdims=True)) + lam_init)
    l0 = jnp.sum(l_ref[0], axis=-1, keepdims=True)
    l1 = jnp.sum(l_ref[1], axis=-1, keepdims=True)
    o = acc_ref[0] / l0 - lam * (acc_ref[1] / l1)
    o_ref[...] = (_rmsnorm(o, subg_ref[...]) * (1.0 - lam_init)).astype(BF16)


def _diff_attention(proj, bias, t5_bias, lam_params, subln_g, lam_init, cast_src=None, *, col0, t):
    S = proj.shape[0]
    HW = 2 * DIFF_D
    qb = col0 // HW
    kb = qb + DIFF_HEADS
    vb = kb + DIFF_HEADS
    grid = (DIFF_HEADS, S // t)
    in_specs = [pl.BlockSpec(memory_space=pltpu.SMEM),
                pl.BlockSpec((4, DIFF_D), lambda h, i: (0, 0)),
                pl.BlockSpec((t, HW), lambda h, i: (i, qb + h)),
                pl.BlockSpec((S, HW), lambda h, i: (0, kb + h)),
                pl.BlockSpec((S, HW), lambda h, i: (0, vb + h)),
                pl.BlockSpec((None, 2, t, t), lambda h, i: (h, 0, 0, 0), pipeline_mode=pl.Buffered(1)),
                pl.BlockSpec((1, HW), lambda h, i: (0, 0))]
    out_specs = [pl.BlockSpec((t, HW), lambda h, i: (i, h))]
    out_shape = [jax.ShapeDtypeStruct((S, DIFF_HEADS * HW), BF16)]
    scratch = [pltpu.VMEM((2, t, LANES), F32), pltpu.VMEM((2, t, LANES), F32), pltpu.VMEM((2, t, HW), F32)]
    args = [t5_bias.T.astype(F32), lam_params, proj, proj, proj, bias, subln_g]
    if cast_src is not None:
        in_spec, out_spec, shape, cast_scratch = _cast_specs(cast_src, grid[0] * grid[1])
        in_specs.append(in_spec)
        out_specs.append(out_spec)
        out_shape.append(shape)
        scratch += cast_scratch
        args.append(cast_src)
    outs = pl.pallas_call(
        functools.partial(_diff_attn_kernel, t=t, lam_init=lam_init, hosts_cast=cast_src is not None),
        grid=grid,
        in_specs=in_specs,
        out_specs=out_specs,
        out_shape=out_shape,
        scratch_shapes=scratch,
        compiler_params=_params("arbitrary", "arbitrary"),
        name="diff_attention",
    )(*args)
    return outs[0] if cast_src is None else tuple(outs)


def _merge_kernel(ya_ref, cb_ref, cc_ref, cu_ref, hc_ref, hu_ref, wconv_ref, yc_ref,
                  g0_ref, g1_ref, g2_ref, wb_ref, o_ref, *, halo):
    i = pl.program_id(0)
    v = cc_ref[...].astype(F32) * cu_ref[...].astype(F32)
    hv = jnp.where(i > 0, hc_ref[...].astype(F32) * hu_ref[...].astype(F32), 0.0)
    prev1 = hv[halo - 1:halo, :]
    prev2 = hv[halo - 2:halo - 1, :]
    row = lax.broadcasted_iota(jnp.int32, v.shape, 0)
    v1 = jnp.where(row == 0, prev1, pltpu.roll(v, 1, 0))
    v2 = jnp.where(row == 0, prev2, jnp.where(row == 1, prev1, pltpu.roll(v, 2, 0)))
    w = wconv_ref[...]
    yb = cb_ref[...].astype(F32) * (w[0:1] * v2 + w[1:2] * v1 + w[2:3] * v)
    merged = jax.nn.sigmoid(g0_ref[...].astype(F32)) * _dot(ya_ref[...], wb_ref[0])
    merged += jax.nn.sigmoid(g1_ref[...].astype(F32)) * _dot(yb.astype(BF16), wb_ref[1])
    merged += jax.nn.sigmoid(g2_ref[...].astype(F32)) * _dot(yc_ref[...], wb_ref[2])
    o_ref[...] = merged.astype(BF16)


def _merge(ya, yc, proj, w_conv, w_branch, *, conv_col0, gate_col0, tm=256, halo=16):
    S = proj.shape[0]
    W = BRANCH_W
    D = w_branch.shape[-1]
    cblk = conv_col0 // W
    gblk = gate_col0 // D
    hb = tm // halo
    row = lambda b: pl.BlockSpec((tm, W), lambda i, b=b: (i, b))
    halo_spec = lambda b: pl.BlockSpec((halo, W), lambda i, b=b: (jnp.maximum(i * hb - 1, 0), b))
    gate = lambda b: pl.BlockSpec((tm, D), lambda i, b=b: (i, b))
    return pl.pallas_call(
        functools.partial(_merge_kernel, halo=halo),
        grid=(S // tm,),
        in_specs=[pl.BlockSpec((tm, W), lambda i: (i, 0)),
                  row(cblk), row(cblk + 1), row(cblk + 2), halo_spec(cblk + 1), halo_spec(cblk + 2),
                  pl.BlockSpec((CONV_K, W), lambda i: (0, 0)),
                  pl.BlockSpec((tm, W), lambda i: (i, 0)),
                  gate(gblk), gate(gblk + 1), gate(gblk + 2),
                  pl.BlockSpec((N_BRANCH, W, D), lambda i: (0, 0, 0))],
        out_specs=pl.BlockSpec((tm, D), lambda i: (i, 0)),
        out_shape=jax.ShapeDtypeStruct((S, D), BF16),
        compiler_params=_params("parallel"),
        name="conv_gated_merge",
    )(ya, proj, proj, proj, proj, proj, w_conv, yc, proj, proj, proj, w_branch)


def _mem_kv_kernel(mem_ref, g_ref, w_ref, o_ref):
    o_ref[...] = _dot(_rmsnorm(mem_ref[...], g_ref[...]).astype(BF16), w_ref[...]).astype(BF16)


def _mem_kv(mem, g, wkv):
    M, D = mem.shape
    N = wkv.shape[1]
    return pl.pallas_call(
        _mem_kv_kernel,
        out_shape=jax.ShapeDtypeStruct((M, N), BF16),
        compiler_params=pltpu.CompilerParams(vmem_limit_bytes=VMEM_LIMIT),
        name="memory_kv",
    )(mem, g, wkv)


def _mix_xattn_kernel(a_ref, wmix_ref, gmix_ref, x_ref, gpre_ref, wq_ref, kv_ref, wo_ref, gpost_ref, o_ref):
    x = x_ref[...] + _rmsnorm(_dot(a_ref[...], wmix_ref[...]), gmix_ref[...])
    q = _dot(_rmsnorm(x, gpre_ref[...]).astype(BF16), wq_ref[...]).astype(BF16)
    width = X_HEADS * X_HEAD_DIM
    outs = []
    for h in range(X_HEADS):
        cols = slice(h * X_HEAD_DIM, (h + 1) * X_HEAD_DIM)
        k = kv_ref[:, cols]
        v = kv_ref[:, width + h * X_HEAD_DIM: width + (h + 1) * X_HEAD_DIM]
        s = _dot_nt(q[:, cols], k)
        p = jnp.exp(s - jnp.max(s, axis=-1, keepdims=True))
        p = p / jnp.sum(p, axis=-1, keepdims=True)
        outs.append(_dot(p.astype(BF16), v).astype(BF16))
    o = jnp.concatenate(outs, axis=-1)
    o_ref[...] = x + _rmsnorm(_dot(o, wo_ref[...]), gpost_ref[...])


def _mix_cross_attention(a, w_mix, g_mix, x, g_pre, wq, kv, wo, g_post, *, tm=512):
    S, D = x.shape
    K = a.shape[1]
    full = lambda arr: pl.BlockSpec(arr.shape, lambda i: (0,) * arr.ndim, pipeline_mode=pl.Buffered(1))
    return pl.pallas_call(
        _mix_xattn_kernel,
        grid=(S // tm,),
        in_specs=[pl.BlockSpec((tm, K), lambda i: (i, 0)), full(w_mix), full(g_mix),
                  pl.BlockSpec((tm, D), lambda i: (i, 0)), full(g_pre), full(wq), full(kv), full(wo), full(g_post)],
        out_specs=pl.BlockSpec((tm, D), lambda i: (i, 0)),
        out_shape=jax.ShapeDtypeStruct((S, D), F32),
        compiler_params=_params("parallel"),
        name="mix_out_cross_attention",
    )(a, w_mix, g_mix, x, g_pre, wq, kv, wo, g_post)


def _swiglu_partial(hn, wg, wu, wd):
    a = _dot(hn, wg)
    u = _dot(hn, wu)
    return _dot((a * jax.nn.sigmoid(a) * u).astype(BF16), wd)


def _ffn_kernel(x_ref, gpre_ref, wg_ref, wu_ref, wd_ref, gpost_ref, o_ref, hn_ref, acc_ref):
    f = pl.program_id(1)

    @pl.when(f == 0)
    def _():
        hn_ref[...] = _rmsnorm(x_ref[...], gpre_ref[...]).astype(BF16)
        acc_ref[...] = jnp.zeros_like(acc_ref)

    acc_ref[...] += _swiglu_partial(hn_ref[...], wg_ref[...], wu_ref[...], wd_ref[...])

    @pl.when(f == pl.num_programs(1) - 1)
    def _():
        o_ref[...] = x_ref[...] + _rmsnorm(acc_ref[...], gpost_ref[...])


def _ffn(x, g_pre, wg, wu, wd, g_post, *, tm=512, tf=FFN_TF):
    S, D = x.shape
    F = wg.shape[1]
    return pl.pallas_call(
        _ffn_kernel,
        grid=(S // tm, F // tf),
        in_specs=[pl.BlockSpec((tm, D), lambda i, f: (i, 0)),
                  pl.BlockSpec((1, D), lambda i, f: (0, 0)),
                  pl.BlockSpec((D, tf), lambda i, f: (0, f)),
                  pl.BlockSpec((D, tf), lambda i, f: (0, f)),
                  pl.BlockSpec((tf, D), lambda i, f: (f, 0)),
                  pl.BlockSpec((1, D), lambda i, f: (0, 0))],
        out_specs=pl.BlockSpec((tm, D), lambda i, f: (i, 0)),
        out_shape=jax.ShapeDtypeStruct((S, D), F32),
        scratch_shapes=[pltpu.VMEM((tm, D), BF16), pltpu.VMEM((tm, D), F32)],
        compiler_params=_params("parallel", "arbitrary"),
        name="swiglu_ffn",
    )(x, g_pre, wg, wu, wd, g_post)


def _router_kernel(x_ref, g_ref, wr_ref, sel_ref, gate_ref):
    hn = _rmsnorm(x_ref[...], g_ref[...])
    logits = jnp.dot(hn, wr_ref[...], preferred_element_type=F32, precision=lax.Precision.HIGHEST)
    lane = lax.broadcasted_iota(jnp.int32, logits.shape, 1)
    neg_inf = -jnp.inf
    lg = jnp.where(lane < N_EXPERTS, logits, neg_inf)
    m1 = jnp.max(lg, axis=-1, keepdims=True)
    i1 = jnp.min(jnp.where(lg == m1, lane, LANES), axis=-1, keepdims=True)
    lg2 = jnp.where(lane == i1, neg_inf, lg)
    m2 = jnp.max(lg2, axis=-1, keepdims=True)
    i2 = jnp.min(jnp.where(lg2 == m2, lane, LANES), axis=-1, keepdims=True)
    e = jnp.exp(m2 - m1)
    g1 = 1.0 / (1.0 + e)
    sel_ref[...] = jnp.where(lane == 0, i1, jnp.where(lane == 1, i2, 0))
    gate_ref[...] = jnp.where(lane == 0, g1, jnp.where(lane == 1, e * g1, 0.0))


def _router(x, g_pre, w_router_padded, *, tm=512):
    S, D = x.shape
    return pl.pallas_call(
        _router_kernel,
        grid=(S // tm,),
        in_specs=[pl.BlockSpec((tm, D), lambda i: (i, 0)),
                  pl.BlockSpec((1, D), lambda i: (0, 0)),
                  pl.BlockSpec((D, LANES), lambda i: (0, 0))],
        out_specs=[pl.BlockSpec((tm, LANES), lambda i: (i, 0)),
                   pl.BlockSpec((tm, LANES), lambda i: (i, 0))],
        out_shape=[jax.ShapeDtypeStruct((S, LANES), jnp.int32),
                   jax.ShapeDtypeStruct((S, LANES), F32)],
        compiler_params=_params("parallel"),
        name="moe_router",
    )(x, g_pre, w_router_padded)


def _moe_plan(sel, gate, tg):
    S = sel.shape[0]
    P, E = 2 * S, N_EXPERTS
    i32 = jnp.int32
    e_flat = sel[:, :2].reshape(P)
    g_flat = gate[:, :2].reshape(P)
    onehot = (e_flat[:, None] == jnp.arange(E, dtype=i32)[None, :]).astype(i32)
    counts = jnp.sum(onehot, axis=0)
    rank = jnp.sum((jnp.cumsum(onehot, axis=0) - 1) * onehot, axis=1)
    ustart = jnp.cumsum(counts) - counts
    tiles = (counts + tg - 1) // tg
    tile_end = jnp.cumsum(tiles)
    tstart = tile_end - tiles
    n_tiles = P // tg + E
    n = jnp.arange(n_tiles, dtype=i32)
    tile_e = jnp.minimum(jnp.sum((n[:, None] >= tile_end[None, :]).astype(i32), axis=1), E - 1)
    used = n < tile_end[-1]
    local = ((n - tstart[tile_e]) * tg)[:, None] + jnp.arange(tg, dtype=i32)[None, :]
    valid = used[:, None] & (local < counts[tile_e][:, None])
    order = jnp.argsort(e_flat, stable=True).astype(i32)
    pair = order[jnp.clip(ustart[tile_e][:, None] + local, 0, P - 1)]
    src_token = jnp.concatenate([jnp.where(valid, pair // 2, 0).reshape(-1), jnp.zeros((tg,), i32)])
    row_gate = jnp.where(valid, g_flat[pair], 0.0).reshape(-1, 1)
    dest = tstart[e_flat] * tg + rank
    return tile_e, used.astype(i32), src_token, row_gate, dest


def _row_gather(idx_ref, base, src_hbm, dst_ref, sem, first, n_rows, *, wait):
    for u in range(n_rows):
        r = first + u
        cp = pltpu.make_async_copy(src_hbm.at[pl.ds(idx_ref[base + r], 1), :], dst_ref.at[pl.ds(r, 1), :], sem)
        if wait:
            cp.wait()
        else:
            cp.start()


def _moe_group_kernel(te_ref, used_ref, src_ref, x_hbm, gpre_ref, gate_ref, wg_ref, wu_ref, wd_ref, y_ref,
                      xg_ref, hb_ref, acc_ref, sem, *, tg, nf):
    n = pl.program_id(0)
    f = pl.program_id(1)
    slot = n % 2
    share = tg // nf
    requested = (n == 0) | (used_ref[jnp.maximum(n - 1, 0)] > 0)

    @pl.when(f == 0)
    def _():
        @pl.when(n == 0)
        def _():
            _row_gather(src_ref, 0, x_hbm, xg_ref.at[0], sem.at[0], 0, tg, wait=False)

        @pl.when(requested)
        def _():
            _row_gather(src_ref, n * tg, x_hbm, xg_ref.at[slot], sem.at[slot], 0, tg, wait=True)
            hb_ref[...] = _rmsnorm(xg_ref[slot], gpre_ref[...]).astype(BF16)

        acc_ref[...] = jnp.zeros_like(acc_ref)

    @pl.when(used_ref[n] > 0)
    def _():
        _row_gather(src_ref, (n + 1) * tg, x_hbm, xg_ref.at[1 - slot], sem.at[1 - slot], f * share, share,
                    wait=False)
        acc_ref[...] += _swiglu_partial(hb_ref[...], wg_ref[...], wu_ref[...], wd_ref[...])

    @pl.when(f == nf - 1)
    def _():
        y_ref[...] = acc_ref[...] * gate_ref[...]

        @pl.when((n == pl.num_programs(0) - 1) & (used_ref[n] > 0))
        def _():
            _row_gather(src_ref, (n + 1) * tg, x_hbm, xg_ref.at[1 - slot], sem.at[1 - slot], 0, tg, wait=True)


def _moe_group(x, g_pre, tile_e, used, src_token, row_gate, wg, wu, wd, *, tg, tf=FFN_TF):
    S, D = x.shape
    E, _, F = wg.shape
    n_tiles = tile_e.shape[0]
    nf = F // tf
    assert tg % nf == 0
    fblk = lambda n, f, used: jnp.where(used[n] > 0, f, nf - 1)
    grid_spec = pltpu.PrefetchScalarGridSpec(
        num_scalar_prefetch=3,
        grid=(n_tiles, nf),
        in_specs=[pl.BlockSpec(memory_space=pl.ANY),
                  pl.BlockSpec((1, D), lambda n, f, te, used, src: (0, 0)),
                  pl.BlockSpec((tg, 1), lambda n, f, te, used, src: (n, 0)),
                  pl.BlockSpec((None, D, tf), lambda n, f, te, used, src: (te[n], 0, fblk(n, f, used))),
                  pl.BlockSpec((None, D, tf), lambda n, f, te, used, src: (te[n], 0, fblk(n, f, used))),
                  pl.BlockSpec((None, tf, D), lambda n, f, te, used, src: (te[n], fblk(n, f, used), 0))],
        out_specs=pl.BlockSpec((tg, D), lambda n, f, te, used, src: (n, 0)),
        scratch_shapes=[pltpu.VMEM((2, tg, D), F32), pltpu.VMEM((tg, D), BF16), pltpu.VMEM((tg, D), F32),
                        pltpu.SemaphoreType.DMA((2,))],
    )
    return pl.pallas_call(
        functools.partial(_moe_group_kernel, tg=tg, nf=nf),
        grid_spec=grid_spec,
        out_shape=jax.ShapeDtypeStruct((n_tiles * tg, D), F32),
        compiler_params=_params("arbitrary", "arbitrary"),
        name="moe_grouped_swiglu",
    )(tile_e, used, src_token, x, g_pre, row_gate, wg, wu, wd)


def _moe_combine_kernel(dest_ref, x_ref, y_hbm, g_ref, o_ref, yb_ref, sem, *, tm):
    i = pl.program_id(0)
    slot = i % 2

    def gather(tile, s, wait):
        base = tile * (2 * tm)
        for r in range(tm):
            for k in range(2):
                row = dest_ref[base + 2 * r + k]
                cp = pltpu.make_async_copy(y_hbm.at[pl.ds(row, 1), :], yb_ref.at[s, k, pl.ds(r, 1), :], sem.at[s])
                if wait:
                    cp.wait()
                else:
                    cp.start(priority=k)

    @pl.when(i == 0)
    def _():
        gather(0, 0, False)

    @pl.when(i + 1 < pl.num_programs(0))
    def _():
        gather(i + 1, 1 - slot, False)

    gather(i, slot, True)
    o_ref[...] = x_ref[...] + _rmsnorm(yb_ref[slot, 0] + yb_ref[slot, 1], g_ref[...])


def _moe_combine(x, y, dest, g_post, *, tm=256):
    S, D = x.shape
    grid_spec = pltpu.PrefetchScalarGridSpec(
        num_scalar_prefetch=1,
        grid=(S // tm,),
        in_specs=[pl.BlockSpec((tm, D), lambda i, dest: (i, 0)),
                  pl.BlockSpec(memory_space=pl.ANY),
                  pl.BlockSpec((1, D), lambda i, dest: (0, 0))],
        out_specs=pl.BlockSpec((tm, D), lambda i, dest: (i, 0)),
        scratch_shapes=[pltpu.VMEM((2, 2, tm, D), F32), pltpu.SemaphoreType.DMA((2,))],
    )
    return pl.pallas_call(
        functools.partial(_moe_combine_kernel, tm=tm),
        grid_spec=grid_spec,
        out_shape=jax.ShapeDtypeStruct((S, D), F32),
        compiler_params=_params("arbitrary"),
        name="moe_combine",
    )(dest, x, y, g_post)


def _rope_tables(S):
    half = RET_D // 2
    inv = ROPE_BASE ** (-jnp.arange(half, dtype=F32) / half)
    ang = jnp.arange(S, dtype=jnp.int32)[:, None].astype(F32) * inv[None, :]
    cos, sin = jnp.cos(ang), jnp.sin(ang)
    return jnp.concatenate([cos, cos], axis=-1), jnp.concatenate([-sin, sin], axis=-1)


def _scaled_in_weights(w):
    ret_w = RET_HEADS * RET_D
    diff_col0 = 4 * ret_w + 3 * CONV_WIDTH
    diff_w = DIFF_HEADS * 2 * DIFF_D
    scale = jnp.ones((w.shape[1],), F32)
    scale = scale.at[:ret_w].set(RET_D ** -0.5)
    scale = scale.at[diff_col0:diff_col0 + diff_w].set(DIFF_D ** -0.5 * LOG2E)
    return (w * scale[None, :]).astype(BF16)


def kernel(x, mem, t5_bias, w_in, w_conv, diff_lambda, diff_subln, w_branch, w_mix_out, w_xq, w_xkv, w_xo, w_ffn_gate, w_ffn_up, w_ffn_down, w_router, w_exp_gate, w_exp_up, w_exp_down, g_pre_mix, g_post_mix, g_pre_xattn, g_mem, g_post_xattn, g_pre_ffn, g_post_ffn):
    B, S, D = x.shape
    assert B == 1
    depth = w_in.shape[0]
    ret_w = RET_HEADS * RET_D
    conv_col0 = 4 * ret_w
    diff_col0 = conv_col0 + 3 * CONV_WIDTH
    gate_col0 = diff_col0 + 3 * DIFF_HEADS * 2 * DIFF_D
    cos2, sin2 = _rope_tables(S)
    near_bias = _near_bias_tiles(t5_bias, ATTN_BLOCK)
    xs = x[0]
    mem2 = mem[0]
    row = lambda g: g[None, :]
    E, _, F = w_exp_gate.shape[1:]
    expert_bf16 = {}
    for l in range(depth):
        moe_next, moe_here = (l + 1) // 2, l // 2
        attn_cast = (w_exp_gate[moe_next].reshape(E * D, F) if l % 2 == 0 and l + 1 < depth
                     else w_exp_up[moe_here].reshape(E * D, F) if l % 2 == 1 else None)
        proj_cast = w_exp_down[moe_here].reshape(E * F, D) if l % 2 == 1 else None
        proj = _in_projection(xs, row(g_pre_mix[l]), _scaled_in_weights(w_in[l]), proj_cast)
        if proj_cast is not None:
            proj, converted = proj
            expert_bf16[moe_here, "down"] = converted.reshape(E, F, D)
        ya = _retention(proj, cos2, sin2)
        lam_init = 0.8 - 0.6 * math.exp(-0.3 * l)
        yc = _diff_attention(proj, near_bias, t5_bias, diff_lambda[l], row(diff_subln[l]), lam_init, attn_cast,
                             col0=diff_col0, t=ATTN_BLOCK)
        if attn_cast is not None:
            yc, converted = yc
            expert_bf16[(moe_next, "gate") if l % 2 == 0 else (moe_here, "up")] = converted.reshape(E, D, F)
        merged = _merge(ya, yc, proj, w_conv[l], w_branch[l].astype(BF16),
                        conv_col0=conv_col0, gate_col0=gate_col0)
        kv = _mem_kv(mem2, row(g_mem[l]), w_xkv[l].astype(BF16))
        xs = _mix_cross_attention(merged, w_mix_out[l].astype(BF16), row(g_post_mix[l]), xs,
                                  row(g_pre_xattn[l]), (w_xq[l] * X_HEAD_DIM ** -0.5).astype(BF16), kv,
                                  w_xo[l].astype(BF16), row(g_post_xattn[l]))
        if l % 2 == 0:
            xs = _ffn(xs, row(g_pre_ffn[l]), w_ffn_gate[l // 2].astype(BF16), w_ffn_up[l // 2].astype(BF16),
                      w_ffn_down[l // 2].astype(BF16), row(g_post_ffn[l]))
        else:
            wr = jnp.pad(w_router[l // 2], ((0, 0), (0, LANES - N_EXPERTS)))
            sel, gate = _router(xs, row(g_pre_ffn[l]), wr)
            tile_e, used, src_token, row_gate, dest = _moe_plan(sel, gate, MOE_TILE)
            y = _moe_group(xs, row(g_pre_ffn[l]), tile_e, used, src_token, row_gate, expert_bf16[moe_here, "gate"],
                           expert_bf16[moe_here, "up"], expert_bf16[moe_here, "down"], tg=MOE_TILE)
            xs = _moe_combine(xs, y, dest, row(g_post_ffn[l]))
    return xs[None]
```
